```python
import jax, jax.numpy as jnp
from jax import lax
import numpy as np

D_MODEL = 2048
BATCH = 2
SEQ = 4096
DEPTH = 2

N_SB_HEADS = 8
SB_HEAD_DIM = 128
SB_WIDTH = N_SB_HEADS * SB_HEAD_DIM
Q_BLOCK = 128
N_SG_GROUPS = 8
SG_GROUP_DIM = 128
SG_WIDTH = N_SG_GROUPS * SG_GROUP_DIM
SG_CHUNK = 128
D_FF_DENSE = 5632
N_EXPERTS = 8
TOP_K = 2
D_FF_EXPERT = 7168
N_MOD = 6
EPS = 1e-6
N_DENSE_LAYERS = (DEPTH + 1) // 2
N_MOE_LAYERS = DEPTH // 2
IN_COLS = 3 * SB_WIDTH + 2 * SG_WIDTH + 2 * D_MODEL
IN_SPLITS = (SB_WIDTH, 2 * SB_WIDTH, 3 * SB_WIDTH, 3 * SB_WIDTH + SG_WIDTH,
             3 * SB_WIDTH + 2 * SG_WIDTH, 3 * SB_WIDTH + 2 * SG_WIDTH + D_MODEL)

kernel_name = "hybrid_stickbreak_spatialgate_moe_adaln"


def rmsnorm(x, g):
    xf = x.astype(jnp.float32)
    y = xf * lax.rsqrt(jnp.mean(xf * xf, axis=-1, keepdims=True) + EPS)
    return y.astype(x.dtype) * g


def stick_breaking_attention(q, k, v):
    S = q.shape[2]
    scale = SB_HEAD_DIM ** -0.5
    outs = []
    for t0 in range(0, S, Q_BLOCK):
        t1 = t0 + Q_BLOCK
        qb = q[:, :, t0:t1]
        kb = k[:, :, :t1]
        vb = v[:, :, :t1]
        z = jnp.einsum('bhqd,bhkd->bhqk', qb, kb).astype(jnp.float32) * scale
        q_pos = t0 + jnp.arange(Q_BLOCK)[:, None]
        k_pos = jnp.arange(t1)[None, :]
        mask = k_pos < q_pos
        log_beta = jnp.where(mask, jax.nn.log_sigmoid(z), -jnp.inf)
        log_1m_beta = jnp.where(mask, jax.nn.log_sigmoid(-z), 0.0)
        log_rest = lax.cumsum(log_1m_beta, axis=3, reverse=True) - log_1m_beta
        w = jnp.exp(log_beta + log_rest)
        outs.append(jnp.einsum('bhqk,bhkd->bhqd', w.astype(vb.dtype), vb))
    return jnp.concatenate(outs, axis=2)


def chunked_spatial_gating(u, vs, sg_norm_g, w_s, b_s):
    B, S, _ = u.shape
    vs = rmsnorm(vs, sg_norm_g)
    vs = vs.reshape(B, S // SG_CHUNK, SG_CHUNK, N_SG_GROUPS, SG_GROUP_DIM)
    causal = jnp.tril(jnp.ones((SG_CHUNK, SG_CHUNK), dtype=bool))
    w = jnp.where(causal[None], w_s, jnp.zeros_like(w_s))
    mixed = jnp.einsum('gts,bnsgc->bntgc', w, vs) + b_s.T[None, None, :, :, None]
    return u * mixed.reshape(B, S, SG_WIDTH)


def hybrid_mixer(h, w_in, q_norm_g, k_norm_g, sg_norm_g, w_s, b_s, w_proj_sb, w_proj_sg, w_out):
    B, S, _ = h.shape
    p = h @ w_in
    q, k, v, su, sv, ga, gb = jnp.split(p, IN_SPLITS, axis=-1)
    heads = lambda t: t.reshape(B, S, N_SB_HEADS, SB_HEAD_DIM)
    q = rmsnorm(heads(q), q_norm_g).transpose(0, 2, 1, 3)
    k = rmsnorm(heads(k), k_norm_g).transpose(0, 2, 1, 3)
    v = heads(v).transpose(0, 2, 1, 3)
    y_sb = stick_breaking_attention(q, k, v).transpose(0, 2, 1, 3).reshape(B, S, SB_WIDTH)
    y_sg = chunked_spatial_gating(jax.nn.gelu(su), jax.nn.gelu(sv), sg_norm_g, w_s, b_s)
    merged = jax.nn.sigmoid(ga) * (y_sb @ w_proj_sb) + jax.nn.sigmoid(gb) * (y_sg @ w_proj_sg)
    return merged @ w_out


def dense_swiglu(h, w1, w3, w2):
    return (jax.nn.silu(h @ w1) * (h @ w3)) @ w2


def moe_swiglu(h, w_router, w1, w3, w2):
    B, S, D = h.shape
    t = h.reshape(B * S, D)
    logits = (t @ w_router).astype(jnp.float32)
    top_val, top_idx = lax.top_k(logits, TOP_K)
    top_w = jax.nn.softmax(top_val, axis=-1)
    gates = jnp.sum(jax.nn.one_hot(top_idx, N_EXPERTS, dtype=jnp.float32) * top_w[..., None], axis=1)
    gates = gates.astype(t.dtype)
    out = jnp.zeros_like(t)
    for e in range(N_EXPERTS):
        he = jax.nn.silu(t @ w1[e]) * (t @ w3[e])
        out = out + gates[:, e:e + 1] * (he @ w2[e])
    return out.reshape(B, S, D)


def setup_inputs(seed: int = 0) -> dict:
    key = jax.random.key(seed)
    ks = iter(jax.random.split(key, 32))
    nrm = lambda shape, std: jax.random.normal(next(ks), shape, jnp.float32) * std
    D = D_MODEL
    return {
        "x": nrm((BATCH, SEQ, D), 1.0),
        "c": nrm((BATCH, D), 1.0),
        "w_ada": nrm((DEPTH, D, N_MOD * D), 0.5 * D ** -0.5),
        "b_ada": nrm((DEPTH, N_MOD * D), 0.02),
        "norm1_g": 1.0 + nrm((DEPTH, D), 0.02),
        "norm2_g": 1.0 + nrm((DEPTH, D), 0.02),
        "w_in": nrm((DEPTH, D, IN_COLS), D ** -0.5),
        "q_norm_g": 1.0 + nrm((DEPTH, SB_HEAD_DIM), 0.02),
        "k_norm_g": 1.0 + nrm((DEPTH, SB_HEAD_DIM), 0.02),
        "sg_norm_g": 1.0 + nrm((DEPTH, SG_WIDTH), 0.02),
        "w_s": nrm((DEPTH, N_SG_GROUPS, SG_CHUNK, SG_CHUNK), 0.5 * SG_CHUNK ** -0.5),
        "b_s": 1.0 + nrm((DEPTH, N_SG_GROUPS, SG_CHUNK), 0.02),
        "w_proj_sb": nrm((DEPTH, SB_WIDTH, D), SB_WIDTH ** -0.5),
        "w_proj_sg": nrm((DEPTH, SG_WIDTH, D), SG_WIDTH ** -0.5),
        "w_out": nrm((DEPTH, D, D), D ** -0.5),
        "ffn_w1": nrm((N_DENSE_LAYERS, D, D_FF_DENSE), D ** -0.5),
        "ffn_w3": nrm((N_DENSE_LAYERS, D, D_FF_DENSE), D ** -0.5),
        "ffn_w2": nrm((N_DENSE_LAYERS, D_FF_DENSE, D), D_FF_DENSE ** -0.5),
        "moe_router": nrm((N_MOE_LAYERS, D, N_EXPERTS), D ** -0.5),
        "moe_w1": nrm((N_MOE_LAYERS, N_EXPERTS, D, D_FF_EXPERT), D ** -0.5),
        "moe_w3": nrm((N_MOE_LAYERS, N_EXPERTS, D, D_FF_EXPERT), D ** -0.5),
        "moe_w2": nrm((N_MOE_LAYERS, N_EXPERTS, D_FF_EXPERT, D), D_FF_EXPERT ** -0.5),
    }


def reference(x, c, w_ada, b_ada, norm1_g, norm2_g, w_in, q_norm_g, k_norm_g, sg_norm_g,
              w_s, b_s, w_proj_sb, w_proj_sg, w_out, ffn_w1, ffn_w3, ffn_w2,
              moe_router, moe_w1, moe_w3, moe_w2):
    c_act = jax.nn.silu(c)
    for i in range(DEPTH):
        mod = (c_act @ w_ada[i] + b_ada[i])[:, None, :]
        shift1, scale1, gate1, shift2, scale2, gate2 = jnp.split(mod, N_MOD, axis=-1)
        h = rmsnorm(x, norm1_g[i]) * (1.0 + scale1) + shift1
        x = x + gate1 * hybrid_mixer(h, w_in[i], q_norm_g[i], k_norm_g[i], sg_norm_g[i],
                                     w_s[i], b_s[i], w_proj_sb[i], w_proj_sg[i], w_out[i])
        h = rmsnorm(x, norm2_g[i]) * (1.0 + scale2) + shift2
        if i % 2 == 0:
            j = i // 2
            f = dense_swiglu(h, ffn_w1[j], ffn_w3[j], ffn_w2[j])
        else:
            j = i // 2
            f = moe_swiglu(h, moe_router[j], moe_w1[j], moe_w3[j], moe_w2[j])
        x = x + gate2 * f
    return x
```

```python
import functools

import jax
import jax.numpy as jnp
from jax import lax
from jax.experimental import pallas as pl
from jax.experimental.pallas import tpu as pltpu

F32 = jnp.float32
BF16 = jnp.bfloat16
EPS = 1e-6
N_MOD = 6
TOP_K = 2
LANES = 128
VMEM_LIMIT = 56 * 1024 * 1024


def _tiles(seq, d_model):
    return dict(
        ada_tn=min(512, d_model),
        in_tm=min(512, seq),
        attn_tq=min(256, seq),
        merge_tm=min(256, seq),
        ffn_tm=min(512, seq),
        ffn_tf=256,
        gather_rows=min(256, seq),
    )


def _cparams(sem):
    return pltpu.CompilerParams(dimension_semantics=sem, vmem_limit_bytes=VMEM_LIMIT)


def _adaln_kernel(cb_ref, w_ref, b_ref, o_ref):
    tn = w_ref.shape[1]
    for b in range(cb_ref.shape[0]):
        cb = cb_ref[b]
        act = cb * jax.nn.sigmoid(cb)
        for j in range(tn // LANES):
            sl = slice(j * LANES, (j + 1) * LANES)
            s = jnp.sum(w_ref[:, sl] * act, axis=0, keepdims=True)
            o_ref[b:b + 1, sl] = s + b_ref[:, sl]


def _adaln(c, w_ada, b_ada, tn):
    depth, k, n = w_ada.shape
    nb = c.shape[0]
    cb = jnp.broadcast_to(c[:, :, None], (nb, k, LANES))
    return pl.pallas_call(
        _adaln_kernel,
        grid=(depth, n // tn),
        in_specs=[
            pl.BlockSpec((nb, k, LANES), lambda l, j: (0, 0, 0)),
            pl.BlockSpec((None, k, tn), lambda l, j: (l, 0, j)),
            pl.BlockSpec((None, 1, tn), lambda l, j: (l, 0, j)),
        ],
        out_specs=pl.BlockSpec((None, nb, tn), lambda l, j: (l, 0, j)),
        out_shape=jax.ShapeDtypeStruct((depth, nb, n), F32),
        compiler_params=_cparams(("arbitrary", "arbitrary")),
        name="adaln",
    )(cb, w_ada, b_ada.reshape(depth, 1, n))


def _in_proj_kernel(x_ref, shift_ref, scale_ref, g_ref, w_ref, qg_ref, kg_ref, sgg_ref, o_ref, h_sc,
                    *, head_dim, qk_scale):
    j = pl.program_id(1)

    @pl.when(j == 0)
    def _():
        x = x_ref[...]
        y = x * lax.rsqrt(jnp.mean(x * x, axis=-1, keepdims=True) + EPS)
        h_sc[...] = ((y * g_ref[...]) * (1.0 + scale_ref[...]) + shift_ref[...]).astype(BF16)

    y = jnp.dot(h_sc[...], w_ref[...], preferred_element_type=F32)
    tn = y.shape[1]

    def head_norm(gain):
        for h in range(tn // head_dim):
            sl = slice(h * head_dim, (h + 1) * head_dim)
            yh = y[:, sl]
            r = lax.rsqrt(jnp.mean(yh * yh, axis=-1, keepdims=True) + EPS)
            o_ref[:, sl] = ((yh * r) * gain).astype(BF16)

    @pl.when(j == 0)
    def _():
        head_norm(qg_ref[...] * qk_scale)

    @pl.when(j == 1)
    def _():
        head_norm(kg_ref[...])

    @pl.when(j == 2)
    def _():
        o_ref[...] = y.astype(BF16)

    @pl.when(j == 3)
    def _():
        o_ref[...] = jax.nn.gelu(y).astype(BF16)

    @pl.when(j == 4)
    def _():
        gv = jax.nn.gelu(y)
        r = lax.rsqrt(jnp.mean(gv * gv, axis=-1, keepdims=True) + EPS)
        o_ref[...] = ((gv * r) * sgg_ref[...]).astype(BF16)

    @pl.when(j >= 5)
    def _():
        o_ref[...] = jax.nn.sigmoid(y).astype(BF16)


def _in_proj(x2, mod_l, norm_g, w_in_bf, q_g, k_g, sg_g, *, seq, tm, width, head_dim):
    t, d = x2.shape
    n = w_in_bf.shape[1]
    tpb = seq // tm
    vec = lambda which: pl.BlockSpec((None, None, 1, d), lambda i, j: (i // tpb, which, 0, 0))
    const = lambda shape: pl.BlockSpec(shape, lambda i, j: (0,) * len(shape))
    kern = functools.partial(_in_proj_kernel, head_dim=head_dim, qk_scale=float(head_dim) ** -0.5)
    return pl.pallas_call(
        kern,
        grid=(t // tm, n // width),
        in_specs=[
            pl.BlockSpec((tm, d), lambda i, j: (i, 0)),
            vec(0), vec(1),
            const((1, d)),
            pl.BlockSpec((d, width), lambda i, j: (0, j)),
            const((1, head_dim)), const((1, head_dim)), const((1, width)),
        ],
        out_specs=pl.BlockSpec((tm, width), lambda i, j: (i, j)),
        out_shape=jax.ShapeDtypeStruct((t, n), BF16),
        scratch_shapes=[pltpu.VMEM((tm, d), BF16)],
        compiler_params=_cparams(("arbitrary", "arbitrary")),
        name="in_proj",
    )(x2, mod_l, mod_l, norm_g.reshape(1, d), w_in_bf, q_g.reshape(1, head_dim), k_g.reshape(1, head_dim),
      sg_g.reshape(1, width))


def _sb_attn_kernel(q_ref, k_ref, v_ref, o_ref):
    i = pl.program_id(2)
    tq, dh = q_ref.shape
    q = q_ref[...]
    row = lax.broadcasted_iota(jnp.int32, (tq, tq), 0)
    col = lax.broadcasted_iota(jnp.int32, (tq, tq), 1)
    suffix = (row >= col).astype(BF16)
    strict = col < row

    def block(kb, carry, acc, diagonal):
        start = pl.multiple_of(kb * tq, tq)
        kblk = k_ref[pl.ds(start, tq), :]
        vblk = v_ref[pl.ds(start, tq), :]
        z = lax.dot_general(q, kblk, (((1,), (1,)), ((), ())), preferred_element_type=F32)
        lm = -(jnp.maximum(z, 0.0) + jnp.log(1.0 + jnp.exp(-jnp.abs(z))))
        if diagonal:
            lm = jnp.where(strict, lm, 0.0)
        hi = lm.astype(BF16)
        lo = (lm - hi.astype(F32)).astype(BF16)
        incl = (jnp.dot(hi, suffix, preferred_element_type=F32)
                + jnp.dot(lo, suffix, preferred_element_type=F32))
        w = jnp.exp(z + incl + carry)
        if diagonal:
            w = jnp.where(strict, w, 0.0)
        acc = acc + jnp.dot(w.astype(BF16), vblk, preferred_element_type=F32)
        return carry + incl[:, 0:1], acc

    carry, acc = block(i, jnp.zeros((tq, 1), F32), jnp.zeros((tq, dh), F32), True)

    def body(s, ca):
        return block(i - 1 - s, ca[0], ca[1], False)

    carry, acc = lax.fori_loop(0, i, body, (carry, acc))
    o_ref[...] = acc.astype(BF16)


def _sb_attention(p, *, batch, seq, n_heads, head_dim, tq):
    t = p.shape[0]
    nq = seq // tq
    return pl.pallas_call(
        _sb_attn_kernel,
        grid=(batch, n_heads, nq),
        in_specs=[
            pl.BlockSpec((tq, head_dim), lambda b, h, i: (b * nq + i, h)),
            pl.BlockSpec((seq, head_dim), lambda b, h, i: (b, n_heads + h)),
            pl.BlockSpec((seq, head_dim), lambda b, h, i: (b, 2 * n_heads + h)),
        ],
        out_specs=pl.BlockSpec((tq, head_dim), lambda b, h, i: (b * nq + i, h)),
        out_shape=jax.ShapeDtypeStruct((t, n_heads * head_dim), BF16),
        compiler_params=_cparams(("arbitrary", "arbitrary", "arbitrary")),
        name="sb_attention",
    )(p, p, p)


def _route_top2(logits):
    n_e = logits.shape[1]
    lane = lax.broadcasted_iota(jnp.int32, logits.shape, 1)
    m1 = jnp.max(logits, axis=-1, keepdims=True)
    i1 = jnp.min(jnp.where(logits == m1, lane, n_e), axis=-1, keepdims=True)
    rest = jnp.where(lane == i1, -jnp.inf, logits)
    m2 = jnp.max(rest, axis=-1, keepdims=True)
    i2 = jnp.min(jnp.where(rest == m2, lane, n_e), axis=-1, keepdims=True)
    t = jnp.exp(m2 - m1)
    w1 = 1.0 / (1.0 + t)
    w2 = t / (1.0 + t)
    out = jnp.where(lane == 0, i1.astype(F32), 0.0)
    out = jnp.where(lane == 1, i2.astype(F32), out)
    out = jnp.where(lane == 2, w1, out)
    out = jnp.where(lane == 3, w2, out)
    return out


def _merge_kernel(*refs, chunk, gdim, moe, dw):
    ysb_ref, su_ref, sv_ref = refs[:3]
    ga_refs = refs[3:3 + dw]
    gb_refs = refs[3 + dw:3 + 2 * dw]
    refs = refs[3 + 2 * dw:]
    x_ref, gate1_ref, shift2_ref, scale2_ref, g2_ref, ws_ref, bs_ref, pa_ref, pb_ref, wo_ref = refs[:10]
    if moe:
        wr_ref, xo_ref, h_ref, route_ref, ysg_sc, m_sc = refs[10:]
    else:
        xo_ref, h_ref, ysg_sc, m_sc = refs[10:]
    tm = x_ref.shape[0]
    width = ysb_ref.shape[1]
    n_groups = ws_ref.shape[0]
    row = lax.broadcasted_iota(jnp.int32, (chunk, chunk), 0)
    col = lax.broadcasted_iota(jnp.int32, (chunk, chunk), 1)
    causal = col <= row
    for g in range(n_groups):
        cs = slice(g * gdim, (g + 1) * gdim)
        wg = jnp.where(causal, ws_ref[g], 0.0).astype(BF16)
        for ch in range(tm // chunk):
            rs = slice(ch * chunk, (ch + 1) * chunk)
            mixed = jnp.dot(wg, sv_ref[rs, cs], preferred_element_type=F32) + bs_ref[g]
            ysg_sc[rs, cs] = (su_ref[rs, cs].astype(F32) * mixed).astype(BF16)
    for s in range(dw):
        sl = slice(s * width, (s + 1) * width)
        a = jnp.dot(ysb_ref[...], pa_ref[:, sl], preferred_element_type=F32)
        b = jnp.dot(ysg_sc[...], pb_ref[:, sl], preferred_element_type=F32)
        m_sc[:, sl] = (ga_refs[s][...].astype(F32) * a + gb_refs[s][...].astype(F32) * b).astype(BF16)
    o = jnp.dot(m_sc[...], wo_ref[...], preferred_element_type=F32)
    xn = x_ref[...] + gate1_ref[...] * o
    xo_ref[...] = xn
    y = xn * lax.rsqrt(jnp.mean(xn * xn, axis=-1, keepdims=True) + EPS)
    h = (y * g2_ref[...]) * (1.0 + scale2_ref[...]) + shift2_ref[...]
    h_ref[...] = h.astype(h_ref.dtype)
    if moe:
        logits = jnp.dot(h, wr_ref[...], preferred_element_type=F32, precision=lax.Precision.HIGHEST)
        route_ref[...] = _route_top2(logits)


def _merge(ysb, p, x2, mod_l, norm2_g, w_s, bs_b, pa_bf, pb_bf, wo_bf, w_router, *, seq, tm, width):
    t, d = x2.shape
    n_groups, chunk, _ = w_s.shape
    gdim = width // n_groups
    tpb = seq // tm
    moe = w_router is not None
    dw = d // width
    vec = lambda which: pl.BlockSpec((None, None, 1, d), lambda i: (i // tpb, which, 0, 0))
    resident = lambda shape: pl.BlockSpec(shape, lambda i: (0,) * len(shape), pipeline_mode=pl.Buffered(1))
    pcol = lambda cb: pl.BlockSpec((tm, width), lambda i: (i, cb))
    in_specs = [
        pl.BlockSpec((tm, width), lambda i: (i, 0)),
        pcol(3),
        pcol(4),
        *[pcol(5 + s) for s in range(dw)],
        *[pcol(5 + dw + s) for s in range(dw)],
        pl.BlockSpec((tm, d), lambda i: (i, 0)),
        vec(2), vec(3), vec(4),
        resident((1, d)),
        resident((n_groups, chunk, chunk)),
        resident((n_groups, chunk, gdim)),
        resident((width, d)), resident((width, d)), resident((d, d)),
    ]
    args = [ysb, p, p, *([p] * (2 * dw)), x2, mod_l, mod_l, mod_l, norm2_g.reshape(1, d), w_s, bs_b,
            pa_bf, pb_bf, wo_bf]
    out_specs = [pl.BlockSpec((tm, d), lambda i: (i, 0)), pl.BlockSpec((tm, d), lambda i: (i, 0))]
    out_shape = [jax.ShapeDtypeStruct((t, d), F32), jax.ShapeDtypeStruct((t, d), F32 if moe else BF16)]
    if moe:
        n_e = w_router.shape[1]
        in_specs.append(resident((d, n_e)))
        args.append(w_router)
        out_specs.append(pl.BlockSpec((tm, n_e), lambda i: (i, 0)))
        out_shape.append(jax.ShapeDtypeStruct((t, n_e), F32))
    kern = functools.partial(_merge_kernel, chunk=chunk, gdim=gdim, moe=moe, dw=dw)
    return pl.pallas_call(
        kern,
        grid=(t // tm,),
        in_specs=in_specs,
        out_specs=out_specs,
        out_shape=out_shape,
        scratch_shapes=[pltpu.VMEM((tm, width), BF16), pltpu.VMEM((tm, d), BF16)],
        compiler_params=_cparams(("arbitrary",)),
        name="merge_moe" if moe else "merge_dense",
    )(*args)


def _ffn_kernel(te_ref, tv_ref, x_ref, w1_ref, w3_ref, w2_ref, *rest, dense):
    del te_ref
    i = pl.program_id(0)
    j = pl.program_id(1)
    last = pl.num_programs(1) - 1
    if dense:
        res_ref, gate_ref, o_ref = rest
    else:
        rowgate_ref, o_ref = rest

    @pl.when(tv_ref[i] > 0)
    def _():
        xb = x_ref[...].astype(BF16)
        a = jnp.dot(xb, w1_ref[...].astype(BF16), preferred_element_type=F32)
        b = jnp.dot(xb, w3_ref[...].astype(BF16), preferred_element_type=F32)
        hh = (jax.nn.silu(a) * b).astype(BF16)
        contrib = jnp.dot(hh, w2_ref[...].astype(BF16), preferred_element_type=F32)

        @pl.when(j == 0)
        def _():
            o_ref[...] = contrib

        @pl.when(j > 0)
        def _():
            o_ref[...] += contrib

        @pl.when(j == last)
        def _():
            if dense:
                o_ref[...] = res_ref[...] + gate_ref[...] * o_ref[...]
            else:
                o_ref[...] = o_ref[...] * rowgate_ref[...]

    @pl.when(jnp.logical_and(tv_ref[i] == 0, j == 0))
    def _():
        o_ref[...] = jnp.zeros_like(o_ref)


def _ffn(xs, w1, w3, w2, tile_e, tile_v, extra, *, tm, tf, dense, seq=None):
    n, d = xs.shape
    n_f = w1.shape[2]
    nj = n_f // tf
    jj = lambda i, j, tv: jnp.where(tv[i] > 0, j, nj - 1)
    in_specs = [
        pl.BlockSpec((tm, d), lambda i, j, te, tv: (i, 0)),
        pl.BlockSpec((None, d, tf), lambda i, j, te, tv: (te[i], 0, jj(i, j, tv))),
        pl.BlockSpec((None, d, tf), lambda i, j, te, tv: (te[i], 0, jj(i, j, tv))),
        pl.BlockSpec((None, tf, d), lambda i, j, te, tv: (te[i], jj(i, j, tv), 0)),
    ]
    if dense:
        res, mod_l = extra
        tpb = seq // tm
        in_specs += [pl.BlockSpec((tm, d), lambda i, j, te, tv: (i, 0)),
                     pl.BlockSpec((None, None, 1, d), lambda i, j, te, tv: (i // tpb, 5, 0, 0))]
        args = (res, mod_l)
    else:
        in_specs += [pl.BlockSpec((tm, 1), lambda i, j, te, tv: (i, 0))]
        args = (extra,)
    return pl.pallas_call(
        functools.partial(_ffn_kernel, dense=dense),
        grid_spec=pltpu.PrefetchScalarGridSpec(
            num_scalar_prefetch=2,
            grid=(n // tm, nj),
            in_specs=in_specs,
            out_specs=pl.BlockSpec((tm, d), lambda i, j, te, tv: (i, 0)),
        ),
        out_shape=jax.ShapeDtypeStruct((n, d), F32),
        compiler_params=_cparams(("arbitrary", "arbitrary")),
        name="ffn_dense" if dense else "ffn_moe",
    )(tile_e, tile_v, xs, w1, w3, w2, *args)


def _row_copy(src_hbm, src_row, dst_ref, dst_row, sem):
    return pltpu.make_async_copy(src_hbm.at[pl.ds(src_row, 1), :], dst_ref.at[pl.ds(dst_row, 1), :], sem)


def _gather_kernel(src_ref, h_hbm, o_ref, sem):
    tg = o_ref.shape[0]
    base = pl.program_id(0) * tg

    def issue(r, c):
        _row_copy(h_hbm, src_ref[base + r], o_ref, r, sem).start()
        return c

    lax.fori_loop(0, tg, issue, 0)

    def wait(r, c):
        _row_copy(h_hbm, 0, o_ref, r, sem).wait()
        return c

    lax.fori_loop(0, tg, wait, 0)


def _gather_rows(src, h, *, tg):
    n = src.shape[0]
    d = h.shape[1]
    return pl.pallas_call(
        _gather_kernel,
        grid_spec=pltpu.PrefetchScalarGridSpec(
            num_scalar_prefetch=1,
            grid=(n // tg,),
            in_specs=[pl.BlockSpec(memory_space=pl.ANY)],
            out_specs=pl.BlockSpec((tg, d), lambda i, s: (i, 0)),
            scratch_shapes=[pltpu.SemaphoreType.DMA(())],
        ),
        out_shape=jax.ShapeDtypeStruct((n, d), h.dtype),
        compiler_params=_cparams(("arbitrary",)),
        name="moe_gather",
    )(src, h)


def _combine_kernel(pos0_ref, pos1_ref, ys_hbm, x_ref, gate_ref, o_ref, buf0, buf1, sem):
    tc = x_ref.shape[0]
    base = pl.program_id(0) * tc

    def issue(r, c):
        _row_copy(ys_hbm, pos0_ref[base + r], buf0, r, sem).start()
        _row_copy(ys_hbm, pos1_ref[base + r], buf1, r, sem).start()
        return c

    lax.fori_loop(0, tc, issue, 0)

    def wait(r, c):
        _row_copy(ys_hbm, 0, buf0, r, sem).wait()
        _row_copy(ys_hbm, 0, buf1, r, sem).wait()
        return c

    lax.fori_loop(0, tc, wait, 0)
    o_ref[...] = x_ref[...] + gate_ref[...] * (buf0[...] + buf1[...])


def _combine(pos0, pos1, ys, x2, mod_l, *, seq, tc):
    t, d = x2.shape
    tpb = seq // tc
    return pl.pallas_call(
        _combine_kernel,
        grid_spec=pltpu.PrefetchScalarGridSpec(
            num_scalar_prefetch=2,
            grid=(t // tc,),
            in_specs=[pl.BlockSpec(memory_space=pl.ANY),
                      pl.BlockSpec((tc, d), lambda i, p0, p1: (i, 0)),
                      pl.BlockSpec((None, None, 1, d), lambda i, p0, p1: (i // tpb, 5, 0, 0))],
            out_specs=pl.BlockSpec((tc, d), lambda i, p0, p1: (i, 0)),
            scratch_shapes=[pltpu.VMEM((tc, d), F32), pltpu.VMEM((tc, d), F32), pltpu.SemaphoreType.DMA(())],
        ),
        out_shape=jax.ShapeDtypeStruct((t, d), F32),
        compiler_params=_cparams(("arbitrary",)),
        name="moe_combine",
    )(pos0, pos1, ys, x2, mod_l)


def _route_metadata(route, n_experts, tm):
    t = route.shape[0]
    e = route[:, :TOP_K].astype(jnp.int32)
    w = route[:, TOP_K:2 * TOP_K]
    cnt = jnp.sum((e[:, :, None] == jnp.arange(n_experts, dtype=jnp.int32)).astype(jnp.int32), axis=1)
    csum = jnp.cumsum(cnt, axis=0)
    rank = csum - cnt
    padded = ((csum[-1] + tm - 1) // tm) * tm
    ends = jnp.cumsum(padded)
    starts = ends - padded
    pos = starts[e] + jnp.take_along_axis(rank, e, axis=1)
    n_tiles = (TOP_K * t + n_experts * (tm - 1)) // tm
    n_pad = n_tiles * tm
    tok = jnp.broadcast_to(jnp.arange(t, dtype=jnp.int32)[:, None], (t, TOP_K))
    flat = pos.reshape(-1)
    src = jnp.zeros((n_pad,), jnp.int32).at[flat].set(tok.reshape(-1), unique_indices=True)
    rowgate = jnp.zeros((n_pad,), F32).at[flat].set(w.reshape(-1), unique_indices=True)
    tile_start = jnp.arange(n_tiles, dtype=jnp.int32) * tm
    tile_e = jnp.minimum(jnp.searchsorted(ends, tile_start, side="right"), n_experts - 1).astype(jnp.int32)
    tile_v = (tile_start < ends[-1]).astype(jnp.int32)
    return pos[:, 0], pos[:, 1], src, rowgate.reshape(n_pad, 1), tile_e, tile_v


def kernel(x, c, w_ada, b_ada, norm1_g, norm2_g, w_in, q_norm_g, k_norm_g, sg_norm_g, w_s, b_s, w_proj_sb,
           w_proj_sg, w_out, ffn_w1, ffn_w3, ffn_w2, moe_router, moe_w1, moe_w3, moe_w2):
    batch, seq, d = x.shape
    depth = w_ada.shape[0]
    head_dim = q_norm_g.shape[1]
    width = w_proj_sb.shape[1]
    n_heads = width // head_dim
    n_groups, chunk = w_s.shape[1], w_s.shape[2]
    gdim = width // n_groups
    n_experts = moe_router.shape[2]
    assert sg_norm_g.shape[1] == width and d % width == 0 and w_in.shape[2] == 5 * width + 2 * d
    assert head_dim % LANES == 0 and gdim % LANES == 0 and n_experts >= 2 * TOP_K
    cfg = _tiles(seq, d)
    t = batch * seq

    mod = _adaln(c, w_ada, b_ada, cfg["ada_tn"]).reshape(depth, batch, N_MOD, 1, d)
    x2 = x.reshape(t, d)
    for i in range(depth):
        mod_l = mod[i]
        p = _in_proj(x2, mod_l, norm1_g[i], w_in[i].astype(BF16), q_norm_g[i], k_norm_g[i], sg_norm_g[i],
                     seq=seq, tm=cfg["in_tm"], width=width, head_dim=head_dim)
        ysb = _sb_attention(p, batch=batch, seq=seq, n_heads=n_heads, head_dim=head_dim, tq=cfg["attn_tq"])
        bs_b = jnp.broadcast_to(b_s[i][:, :, None], (n_groups, chunk, gdim))
        moe = i % 2 == 1
        outs = _merge(ysb, p, x2, mod_l, norm2_g[i], w_s[i], bs_b, w_proj_sb[i].astype(BF16),
                      w_proj_sg[i].astype(BF16), w_out[i].astype(BF16), moe_router[i // 2] if moe else None,
                      seq=seq, tm=cfg["merge_tm"], width=width)
        if not moe:
            x2, h = outs
            jd = i // 2
            n_tiles = t // cfg["ffn_tm"]
            x2 = _ffn(h, ffn_w1[jd][None], ffn_w3[jd][None], ffn_w2[jd][None],
                      jnp.zeros((n_tiles,), jnp.int32), jnp.ones((n_tiles,), jnp.int32), (x2, mod_l),
                      tm=cfg["ffn_tm"], tf=cfg["ffn_tf"], dense=True, seq=seq)
        else:
            x2, h, route = outs
            jm = i // 2
            pos0, pos1, src, rowgate, tile_e, tile_v = _route_metadata(route, n_experts, cfg["ffn_tm"])
            xs = _gather_rows(src, h, tg=cfg["gather_rows"])
            ys = _ffn(xs, moe_w1[jm], moe_w3[jm], moe_w2[jm], tile_e, tile_v, rowgate,
                      tm=cfg["ffn_tm"], tf=cfg["ffn_tf"], dense=False)
            x2 = _combine(pos0, pos1, ys, x2, mod_l, seq=seq, tc=cfg["gather_rows"])
    return x2.reshape(batch, seq, d)
```

```python
import functools

import jax
import jax.numpy as jnp
from jax import lax
from jax.experimental import pallas as pl
from jax.experimental.pallas import tpu as pltpu

F32 = jnp.float32
BF16 = jnp.bfloat16
EPS = 1e-6
N_MOD = 6
TOP_K = 2
LOG2E = 1.4426950408889634
ATTN_UNROLL = 4
LANES = 128
VMEM_LIMIT = 56 * 1024 * 1024


def _tiles(seq, d_model):
    return dict(
        ada_tn=min(512, d_model),
        in_tm=min(512, seq),
        attn_tq=min(256, seq),
        merge_tm=min(256, seq),
        ffn_tm=min(1024, seq),
        ffn_sub=min(256, seq),
        ffn_tf=256,
        gather_rows=min(256, seq),
    )


def _cparams(sem):
    return pltpu.CompilerParams(dimension_semantics=sem, vmem_limit_bytes=VMEM_LIMIT)


def _adaln_kernel(cb_ref, w_ref, b_ref, o_ref):
    tn = w_ref.shape[1]
    for b in range(cb_ref.shape[0]):
        cb = cb_ref[b]
        act = cb * jax.nn.sigmoid(cb)
        for j in range(tn // LANES):
            sl = slice(j * LANES, (j + 1) * LANES)
            s = jnp.sum(w_ref[:, sl] * act, axis=0, keepdims=True)
            o_ref[b:b + 1, sl] = s + b_ref[:, sl]


def _adaln(c, w_ada, b_ada, tn):
    depth, k, n = w_ada.shape
    nb = c.shape[0]
    cb = jnp.broadcast_to(c[:, :, None], (nb, k, LANES))
    return pl.pallas_call(
        _adaln_kernel,
        grid=(depth, n // tn),
        in_specs=[
            pl.BlockSpec((nb, k, LANES), lambda l, j: (0, 0, 0)),
            pl.BlockSpec((None, k, tn), lambda l, j: (l, 0, j)),
            pl.BlockSpec((None, 1, tn), lambda l, j: (l, 0, j)),
        ],
        out_specs=pl.BlockSpec((None, nb, tn), lambda l, j: (l, 0, j)),
        out_shape=jax.ShapeDtypeStruct((depth, nb, n), F32),
        compiler_params=_cparams(("arbitrary", "arbitrary")),
        name="adaln",
    )(cb, w_ada, b_ada.reshape(depth, 1, n))


def _in_proj_kernel(x_ref, shift_ref, scale_ref, g_ref, w_ref, qg_ref, kg_ref, sgg_ref, o_ref, h_sc,
                    *, head_dim, qk_scale):
    j = pl.program_id(1)

    @pl.when(j == 0)
    def _():
        x = x_ref[...]
        y = x * lax.rsqrt(jnp.mean(x * x, axis=-1, keepdims=True) + EPS)
        h_sc[...] = ((y * g_ref[...]) * (1.0 + scale_ref[...]) + shift_ref[...]).astype(BF16)

    y = jnp.dot(h_sc[...], w_ref[...], preferred_element_type=F32)
    tn = y.shape[1]

    def head_norm(gain):
        for h in range(tn // head_dim):
            sl = slice(h * head_dim, (h + 1) * head_dim)
            yh = y[:, sl]
            r = lax.rsqrt(jnp.mean(yh * yh, axis=-1, keepdims=True) + EPS)
            o_ref[:, sl] = ((yh * r) * gain).astype(BF16)

    @pl.when(j == 0)
    def _():
        head_norm(qg_ref[...] * qk_scale)

    @pl.when(j == 1)
    def _():
        head_norm(kg_ref[...])

    @pl.when(j == 2)
    def _():
        o_ref[...] = y.astype(BF16)

    @pl.when(j == 3)
    def _():
        o_ref[...] = jax.nn.gelu(y).astype(BF16)

    @pl.when(j == 4)
    def _():
        gv = jax.nn.gelu(y)
        r = lax.rsqrt(jnp.mean(gv * gv, axis=-1, keepdims=True) + EPS)
        o_ref[...] = ((gv * r) * sgg_ref[...]).astype(BF16)

    @pl.when(j >= 5)
    def _():
        o_ref[...] = jax.nn.sigmoid(y).astype(BF16)


def _in_proj(x2, mod_l, norm_g, w_in_bf, q_g, k_g, sg_g, *, seq, tm, width, head_dim):
    t, d = x2.shape
    n = w_in_bf.shape[1]
    tpb = seq // tm
    vec = lambda which: pl.BlockSpec((None, None, 1, d), lambda i, j: (i // tpb, which, 0, 0))
    const = lambda shape: pl.BlockSpec(shape, lambda i, j: (0,) * len(shape))
    kern = functools.partial(_in_proj_kernel, head_dim=head_dim, qk_scale=float(head_dim) ** -0.5 * LOG2E)
    return pl.pallas_call(
        kern,
        grid=(t // tm, n // width),
        in_specs=[
            pl.BlockSpec((tm, d), lambda i, j: (i, 0)),
            vec(0), vec(1),
            const((1, d)),
            pl.BlockSpec((d, width), lambda i, j: (0, j)),
            const((1, head_dim)), const((1, head_dim)), const((1, width)),
        ],
        out_specs=pl.BlockSpec((tm, width), lambda i, j: (i, j)),
        out_shape=jax.ShapeDtypeStruct((t, n), BF16),
        scratch_shapes=[pltpu.VMEM((tm, d), BF16)],
        compiler_params=_cparams(("arbitrary", "arbitrary")),
        name="in_proj",
    )(x2, mod_l, mod_l, norm_g.reshape(1, d), w_in_bf, q_g.reshape(1, head_dim), k_g.reshape(1, head_dim),
      sg_g.reshape(1, width))


def _sb_attn_kernel(q_ref, k_ref, v_ref, o_ref, carry_sc, acc_sc, *, unroll):
    i = pl.program_id(2)
    tq, dh = q_ref.shape
    q = q_ref[...]
    row = lax.broadcasted_iota(jnp.int32, (tq, tq), 0)
    col = lax.broadcasted_iota(jnp.int32, (tq, tq), 1)
    suffix = (row >= col).astype(BF16)
    strict = col < row

    def scores(kb, diagonal):
        start = pl.multiple_of(kb * tq, tq)
        z = lax.dot_general(q, k_ref[pl.ds(start, tq), :], (((1,), (1,)), ((), ())),
                            preferred_element_type=F32)
        sp = jnp.maximum(z, 0.0) + jnp.log2(1.0 + jnp.exp2(-jnp.abs(z)))
        if diagonal:
            sp = jnp.where(strict, sp, 0.0)
        incl = jnp.dot(sp.astype(BF16), suffix, preferred_element_type=F32)
        return start, z, incl, diagonal

    def accumulate(parts, carry, acc):
        for start, z, incl, diagonal in parts:
            w = jnp.exp2(z - incl - carry)
            if diagonal:
                w = jnp.where(strict, w, 0.0)
            acc = acc + jnp.dot(w.astype(BF16), v_ref[pl.ds(start, tq), :], preferred_element_type=F32)
            carry = carry + incl[:, 0:1]
        return carry, acc

    rem = i % unroll
    for r in range(unroll):
        @pl.when(rem == r)
        def _(r=r):
            parts = [scores(i, True)] + [scores(i - 1 - u, False) for u in range(r)]
            carry, acc = accumulate(parts, jnp.zeros((tq, 1), F32), jnp.zeros((tq, dh), F32))
            carry_sc[...] = carry
            acc_sc[...] = acc

    def group(s, ca):
        first = i - rem - 1 - s * unroll
        return accumulate([scores(first - u, False) for u in range(unroll)], ca[0], ca[1])

    carry, acc = lax.fori_loop(0, i // unroll, group, (carry_sc[...], acc_sc[...]))
    o_ref[...] = acc.astype(BF16)


def _sb_attention(p, *, batch, seq, n_heads, head_dim, tq):
    t = p.shape[0]
    nq = seq // tq
    return pl.pallas_call(
        functools.partial(_sb_attn_kernel, unroll=ATTN_UNROLL),
        grid=(batch, n_heads, nq),
        in_specs=[
            pl.BlockSpec((tq, head_dim), lambda b, h, i: (b * nq + i, h)),
            pl.BlockSpec((seq, head_dim), lambda b, h, i: (b, n_heads + h)),
            pl.BlockSpec((seq, head_dim), lambda b, h, i: (b, 2 * n_heads + h)),
        ],
        out_specs=pl.BlockSpec((tq, head_dim), lambda b, h, i: (b * nq + i, h)),
        out_shape=jax.ShapeDtypeStruct((t, n_heads * head_dim), BF16),
        scratch_shapes=[pltpu.VMEM((tq, 1), F32), pltpu.VMEM((tq, head_dim), F32)],
        compiler_params=_cparams(("arbitrary", "arbitrary", "arbitrary")),
        name="sb_attention",
    )(p, p, p)


def _route_top2(logits):
    n_e = logits.shape[1]
    lane = lax.broadcasted_iota(jnp.int32, logits.shape, 1)
    m1 = jnp.max(logits, axis=-1, keepdims=True)
    i1 = jnp.min(jnp.where(logits == m1, lane, n_e), axis=-1, keepdims=True)
    rest = jnp.where(lane == i1, -jnp.inf, logits)
    m2 = jnp.max(rest, axis=-1, keepdims=True)
    i2 = jnp.min(jnp.where(rest == m2, lane, n_e), axis=-1, keepdims=True)
    t = jnp.exp(m2 - m1)
    w1 = 1.0 / (1.0 + t)
    w2 = t / (1.0 + t)
    out = jnp.where(lane == 0, i1.astype(F32), 0.0)
    out = jnp.where(lane == 1, i2.astype(F32), out)
    out = jnp.where(lane == 2, w1, out)
    out = jnp.where(lane == 3, w2, out)
    return out


def _merge_kernel(*refs, chunk, gdim, moe, dw):
    ysb_ref, su_ref, sv_ref = refs[:3]
    ga_refs = refs[3:3 + dw]
    gb_refs = refs[3 + dw:3 + 2 * dw]
    refs = refs[3 + 2 * dw:]
    x_ref, gate1_ref, shift2_ref, scale2_ref, g2_ref, ws_ref, bs_ref, pa_ref, pb_ref, wo_ref = refs[:10]
    if moe:
        wr_ref, xo_ref, h_ref, route_ref, ysg_sc, m_sc = refs[10:]
    else:
        xo_ref, h_ref, ysg_sc, m_sc = refs[10:]
    tm = x_ref.shape[0]
    width = ysb_ref.shape[1]
    n_groups = ws_ref.shape[0]
    row = lax.broadcasted_iota(jnp.int32, (chunk, chunk), 0)
    col = lax.broadcasted_iota(jnp.int32, (chunk, chunk), 1)
    causal = col <= row
    for g in range(n_groups):
        cs = slice(g * gdim, (g + 1) * gdim)
        wg = jnp.where(causal, ws_ref[g], 0.0).astype(BF16)
        for ch in range(tm // chunk):
            rs = slice(ch * chunk, (ch + 1) * chunk)
            mixed = jnp.dot(wg, sv_ref[rs, cs], preferred_element_type=F32) + bs_ref[g]
            ysg_sc[rs, cs] = (su_ref[rs, cs].astype(F32) * mixed).astype(BF16)
    for s in range(dw):
        sl = slice(s * width, (s + 1) * width)
        a = jnp.dot(ysb_ref[...], pa_ref[:, sl], preferred_element_type=F32)
        b = jnp.dot(ysg_sc[...], pb_ref[:, sl], preferred_element_type=F32)
        m_sc[:, sl] = (ga_refs[s][...].astype(F32) * a + gb_refs[s][...].astype(F32) * b).astype(BF16)
    o = jnp.dot(m_sc[...], wo_ref[...], preferred_element_type=F32)
    xn = x_ref[...] + gate1_ref[...] * o
    xo_ref[...] = xn
    y = xn * lax.rsqrt(jnp.mean(xn * xn, axis=-1, keepdims=True) + EPS)
    h = (y * g2_ref[...]) * (1.0 + scale2_ref[...]) + shift2_ref[...]
    h_ref[...] = h.astype(h_ref.dtype)
    if moe:
        wr = wr_ref[...]
        h_hi, w_hi = h.astype(BF16), wr.astype(BF16)
        h_lo = (h - h_hi.astype(F32)).astype(BF16)
        w_lo = (wr - w_hi.astype(F32)).astype(BF16)
        logits = (jnp.dot(h_hi, w_hi, preferred_element_type=F32)
                  + jnp.dot(h_lo, w_hi, preferred_element_type=F32)
                  + jnp.dot(h_hi, w_lo, preferred_element_type=F32))
        route_ref[...] = _route_top2(logits)


def _merge(ysb, p, x2, mod_l, norm2_g, w_s, bs_b, pa_bf, pb_bf, wo_bf, w_router, *, seq, tm, width):
    t, d = x2.shape
    n_groups, chunk, _ = w_s.shape
    gdim = width // n_groups
    tpb = seq // tm
    moe = w_router is not None
    dw = d // width
    vec = lambda which: pl.BlockSpec((None, None, 1, d), lambda i: (i // tpb, which, 0, 0))
    resident = lambda shape: pl.BlockSpec(shape, lambda i: (0,) * len(shape), pipeline_mode=pl.Buffered(1))
    pcol = lambda cb: pl.BlockSpec((tm, width), lambda i: (i, cb))
    in_specs = [
        pl.BlockSpec((tm, width), lambda i: (i, 0)),
        pcol(3),
        pcol(4),
        *[pcol(5 + s) for s in range(dw)],
        *[pcol(5 + dw + s) for s in range(dw)],
        pl.BlockSpec((tm, d), lambda i: (i, 0)),
        vec(2), vec(3), vec(4),
        resident((1, d)),
        resident((n_groups, chunk, chunk)),
        resident((n_groups, chunk, gdim)),
        resident((width, d)), resident((width, d)), resident((d, d)),
    ]
    args = [ysb, p, p, *([p] * (2 * dw)), x2, mod_l, mod_l, mod_l, norm2_g.reshape(1, d), w_s, bs_b,
            pa_bf, pb_bf, wo_bf]
    out_specs = [pl.BlockSpec((tm, d), lambda i: (i, 0)), pl.BlockSpec((tm, d), lambda i: (i, 0))]
    out_shape = [jax.ShapeDtypeStruct((t, d), F32), jax.ShapeDtypeStruct((t, d), F32 if moe else BF16)]
    if moe:
        n_e = w_router.shape[1]
        in_specs.append(resident((d, n_e)))
        args.append(w_router)
        out_specs.append(pl.BlockSpec((tm, n_e), lambda i: (i, 0)))
        out_shape.append(jax.ShapeDtypeStruct((t, n_e), F32))
    kern = functools.partial(_merge_kernel, chunk=chunk, gdim=gdim, moe=moe, dw=dw)
    return pl.pallas_call(
        kern,
        grid=(t // tm,),
        in_specs=in_specs,
        out_specs=out_specs,
        out_shape=out_shape,
        scratch_shapes=[pltpu.VMEM((tm, width), BF16), pltpu.VMEM((tm, d), BF16)],
        compiler_params=_cparams(("arbitrary",)),
        name="merge_moe" if moe else "merge_dense",
    )(*args)


def _ffn_kernel(te_ref, tv_ref, x_ref, w1_ref, w3_ref, w2_ref, *rest, dense, sub):
    del te_ref
    i = pl.program_id(0)
    j = pl.program_id(1)
    last = pl.num_programs(1) - 1
    if dense:
        res_ref, gate_ref, o_ref, xb_sc, w1_sc, w3_sc, w2_sc = rest
    else:
        rowgate_ref, o_ref, xb_sc, w1_sc, w3_sc, w2_sc = rest
    n_sub = tv_ref[i]

    @pl.when(j == 0)
    def _():
        xb_sc[...] = x_ref[...].astype(BF16)
        o_ref[...] = jnp.zeros_like(o_ref)

    @pl.when(n_sub > 0)
    def _():
        w1_sc[...] = w1_ref[...].astype(BF16)
        w3_sc[...] = w3_ref[...].astype(BF16)
        w2_sc[...] = w2_ref[...].astype(BF16)

        def rows(s):
            r0 = pl.multiple_of(s * sub, sub)
            xs = xb_sc[pl.ds(r0, sub), :]
            a = jnp.dot(xs, w1_sc[...], preferred_element_type=F32)
            b = jnp.dot(xs, w3_sc[...], preferred_element_type=F32)
            hh = (jax.nn.silu(a) * b).astype(BF16)
            o_ref[pl.ds(r0, sub), :] += jnp.dot(hh, w2_sc[...], preferred_element_type=F32)

        def pair(s, c):
            rows(2 * s)
            rows(2 * s + 1)
            return c

        lax.fori_loop(0, n_sub // 2, pair, 0)

        @pl.when(n_sub % 2 == 1)
        def _():
            rows(n_sub - 1)

    @pl.when(j == last)
    def _():
        if dense:
            o_ref[...] = res_ref[...] + gate_ref[...] * o_ref[...]
        else:
            o_ref[...] = o_ref[...] * rowgate_ref[...]


def _ffn(xs, w1, w3, w2, tile_e, tile_v, extra, *, tm, tf, sub, dense, seq=None):
    n, d = xs.shape
    n_f = w1.shape[2]
    nj = n_f // tf
    jj = lambda i, j, tv: jnp.where(tv[i] > 0, j, nj - 1)
    once = pl.Buffered(1)
    in_specs = [
        pl.BlockSpec((tm, d), lambda i, j, te, tv: (i, 0), pipeline_mode=once),
        pl.BlockSpec((None, d, tf), lambda i, j, te, tv: (te[i], 0, jj(i, j, tv))),
        pl.BlockSpec((None, d, tf), lambda i, j, te, tv: (te[i], 0, jj(i, j, tv))),
        pl.BlockSpec((None, tf, d), lambda i, j, te, tv: (te[i], jj(i, j, tv), 0)),
    ]
    if dense:
        res, mod_l = extra
        tpb = seq // tm
        in_specs += [pl.BlockSpec((tm, d), lambda i, j, te, tv: (i, 0), pipeline_mode=once),
                     pl.BlockSpec((None, None, 1, d), lambda i, j, te, tv: (i // tpb, 5, 0, 0))]
        args = (res, mod_l)
    else:
        in_specs += [pl.BlockSpec((tm, 1), lambda i, j, te, tv: (i, 0))]
        args = (extra,)
    return pl.pallas_call(
        functools.partial(_ffn_kernel, dense=dense, sub=sub),
        grid_spec=pltpu.PrefetchScalarGridSpec(
            num_scalar_prefetch=2,
            grid=(n // tm, nj),
            in_specs=in_specs,
            out_specs=pl.BlockSpec((tm, d), lambda i, j, te, tv: (i, 0)),
            scratch_shapes=[pltpu.VMEM((tm, d), BF16), pltpu.VMEM((d, tf), BF16), pltpu.VMEM((d, tf), BF16),
                            pltpu.VMEM((tf, d), BF16)],
        ),
        out_shape=jax.ShapeDtypeStruct((n, d), F32),
        compiler_params=_cparams(("arbitrary", "arbitrary")),
        name="ffn_dense" if dense else "ffn_moe",
    )(tile_e, tile_v, xs, w1, w3, w2, *args)


def _row_copy(src_hbm, src_row, dst_ref, dst_row, sem):
    return pltpu.make_async_copy(src_hbm.at[pl.ds(src_row, 1), :], dst_ref.at[pl.ds(dst_row, 1), :], sem)


def _gather_kernel(src_ref, h_hbm, o_ref, sem):
    tg = o_ref.shape[0]
    base = pl.program_id(0) * tg

    def issue(r, c):
        _row_copy(h_hbm, src_ref[base + r], o_ref, r, sem).start()
        return c

    lax.fori_loop(0, tg, issue, 0)

    def wait(r, c):
        _row_copy(h_hbm, 0, o_ref, r, sem).wait()
        return c

    lax.fori_loop(0, tg, wait, 0)


def _gather_rows(src, h, *, tg):
    n = src.shape[0]
    d = h.shape[1]
    return pl.pallas_call(
        _gather_kernel,
        grid_spec=pltpu.PrefetchScalarGridSpec(
            num_scalar_prefetch=1,
            grid=(n // tg,),
            in_specs=[pl.BlockSpec(memory_space=pl.ANY)],
            out_specs=pl.BlockSpec((tg, d), lambda i, s: (i, 0)),
            scratch_shapes=[pltpu.SemaphoreType.DMA(())],
        ),
        out_shape=jax.ShapeDtypeStruct((n, d), h.dtype),
        compiler_params=_cparams(("arbitrary",)),
        name="moe_gather",
    )(src, h)


def _combine_kernel(pos0_ref, pos1_ref, ys_hbm, x_ref, gate_ref, o_ref, buf0, buf1, sem):
    tc = x_ref.shape[0]
    base = pl.program_id(0) * tc

    def issue(r, c):
        _row_copy(ys_hbm, pos0_ref[base + r], buf0, r, sem).start()
        _row_copy(ys_hbm, pos1_ref[base + r], buf1, r, sem).start()
        return c

    lax.fori_loop(0, tc, issue, 0)

    def wait(r, c):
        _row_copy(ys_hbm, 0, buf0, r, sem).wait()
        _row_copy(ys_hbm, 0, buf1, r, sem).wait()
        return c

    lax.fori_loop(0, tc, wait, 0)
    o_ref[...] = x_ref[...] + gate_ref[...] * (buf0[...] + buf1[...])


def _combine(pos0, pos1, ys, x2, mod_l, *, seq, tc):
    t, d = x2.shape
    tpb = seq // tc
    return pl.pallas_call(
        _combine_kernel,
        grid_spec=pltpu.PrefetchScalarGridSpec(
            num_scalar_prefetch=2,
            grid=(t // tc,),
            in_specs=[pl.BlockSpec(memory_space=pl.ANY),
                      pl.BlockSpec((tc, d), lambda i, p0, p1: (i, 0)),
                      pl.BlockSpec((None, None, 1, d), lambda i, p0, p1: (i // tpb, 5, 0, 0))],
            out_specs=pl.BlockSpec((tc, d), lambda i, p0, p1: (i, 0)),
            scratch_shapes=[pltpu.VMEM((tc, d), F32), pltpu.VMEM((tc, d), F32), pltpu.SemaphoreType.DMA(())],
        ),
        out_shape=jax.ShapeDtypeStruct((t, d), F32),
        compiler_params=_cparams(("arbitrary",)),
        name="moe_combine",
    )(pos0, pos1, ys, x2, mod_l)


def _route_metadata(route, n_experts, tm, sub):
    t = route.shape[0]
    e = route[:, :TOP_K].astype(jnp.int32)
    w = route[:, TOP_K:2 * TOP_K]
    cnt = jnp.sum((e[:, :, None] == jnp.arange(n_experts, dtype=jnp.int32)).astype(jnp.int32), axis=1)
    csum = jnp.cumsum(cnt, axis=0)
    rank = csum - cnt
    counts = csum[-1]
    padded = ((counts + tm - 1) // tm) * tm
    ends = jnp.cumsum(padded)
    starts = ends - padded
    pos = starts[e] + jnp.take_along_axis(rank, e, axis=1)
    n_tiles = (TOP_K * t + n_experts * (tm - 1)) // tm
    n_pad = n_tiles * tm
    tok = jnp.broadcast_to(jnp.arange(t, dtype=jnp.int32)[:, None], (t, TOP_K))
    flat = pos.reshape(-1)
    src = jnp.zeros((n_pad,), jnp.int32).at[flat].set(tok.reshape(-1), unique_indices=True)
    rowgate = jnp.zeros((n_pad,), F32).at[flat].set(w.reshape(-1), unique_indices=True)
    tile_start = jnp.arange(n_tiles, dtype=jnp.int32) * tm
    tile_e = jnp.sum((tile_start[:, None] >= ends[None, :]).astype(jnp.int32), axis=1)
    tile_e = jnp.minimum(tile_e, n_experts - 1)
    real_rows = jnp.clip(starts[tile_e] + counts[tile_e] - tile_start, 0, tm)
    tile_v = ((real_rows + sub - 1) // sub).astype(jnp.int32)
    return pos[:, 0], pos[:, 1], src, rowgate.reshape(n_pad, 1), tile_e, tile_v


def kernel(x, c, w_ada, b_ada, norm1_g, norm2_g, w_in, q_norm_g, k_norm_g, sg_norm_g, w_s, b_s, w_proj_sb,
           w_proj_sg, w_out, ffn_w1, ffn_w3, ffn_w2, moe_router, moe_w1, moe_w3, moe_w2):
    batch, seq, d = x.shape
    depth = w_ada.shape[0]
    head_dim = q_norm_g.shape[1]
    width = w_proj_sb.shape[1]
    n_heads = width // head_dim
    n_groups, chunk = w_s.shape[1], w_s.shape[2]
    gdim = width // n_groups
    n_experts = moe_router.shape[2]
    assert sg_norm_g.shape[1] == width and d % width == 0 and w_in.shape[2] == 5 * width + 2 * d
    assert head_dim % LANES == 0 and gdim % LANES == 0 and n_experts >= 2 * TOP_K
    cfg = _tiles(seq, d)
    t = batch * seq

    mod = _adaln(c, w_ada, b_ada, cfg["ada_tn"]).reshape(depth, batch, N_MOD, 1, d)
    x2 = x.reshape(t, d)
    for i in range(depth):
        mod_l = mod[i]
        p = _in_proj(x2, mod_l, norm1_g[i], w_in[i].astype(BF16), q_norm_g[i], k_norm_g[i], sg_norm_g[i],
                     seq=seq, tm=cfg["in_tm"], width=width, head_dim=head_dim)
        ysb = _sb_attention(p, batch=batch, seq=seq, n_heads=n_heads, head_dim=head_dim, tq=cfg["attn_tq"])
        bs_b = jnp.broadcast_to(b_s[i][:, :, None], (n_groups, chunk, gdim))
        moe = i % 2 == 1
        outs = _merge(ysb, p, x2, mod_l, norm2_g[i], w_s[i], bs_b, w_proj_sb[i].astype(BF16),
                      w_proj_sg[i].astype(BF16), w_out[i].astype(BF16), moe_router[i // 2] if moe else None,
                      seq=seq, tm=cfg["merge_tm"], width=width)
        if not moe:
            x2, h = outs
            jd = i // 2
            n_tiles = t // cfg["ffn_tm"]
            x2 = _ffn(h, ffn_w1[jd][None], ffn_w3[jd][None], ffn_w2[jd][None],
                      jnp.zeros((n_tiles,), jnp.int32),
                      jnp.full((n_tiles,), cfg["ffn_tm"] // cfg["ffn_sub"], jnp.int32), (x2, mod_l),
                      tm=cfg["ffn_tm"], tf=cfg["ffn_tf"], sub=cfg["ffn_sub"], dense=True, seq=seq)
        else:
            x2, h, route = outs
            jm = i // 2
            pos0, pos1, src, rowgate, tile_e, tile_v = _route_metadata(route, n_experts, cfg["ffn_tm"],
                                                                       cfg["ffn_sub"])
            xs = _gather_rows(src, h, tg=cfg["gather_rows"])
            ys = _ffn(xs, moe_w1[jm], moe_w3[jm], moe_w2[jm], tile_e, tile_v, rowgate,
                      tm=cfg["ffn_tm"], tf=cfg["ffn_tf"], sub=cfg["ffn_sub"], dense=False)
            x2 = _combine(pos0, pos1, ys, x2, mod_l, seq=seq, tc=cfg["gather_rows"])
    return x2.reshape(batch, seq, d)
```

```python
import functools

import jax
import jax.numpy as jnp
from jax import lax
from jax.experimental import pallas as pl
from jax.experimental.pallas import tpu as pltpu

F32 = jnp.float32
BF16 = jnp.bfloat16
EPS = 1e-6
N_MOD = 6
TOP_K = 2
LOG2E = 1.4426950408889634
ATTN_UNROLL = 8
LANES = 128
MXU_N = 256
VMEM_LIMIT = 56 * 1024 * 1024


def _tiles(seq, d_model):
    return dict(
        ada_tn=min(512, d_model),
        in_tm=min(1024, seq),
        attn_tq=min(256, seq),
        merge_tm=min(256, seq),
        ffn_tm=min(1024, seq),
        ffn_sub=min(512, seq),
        ffn_tf=256,
        gather_rows=min(256, seq),
    )


def _cparams(sem):
    return pltpu.CompilerParams(dimension_semantics=sem, vmem_limit_bytes=VMEM_LIMIT)


def _adaln_kernel(cb_ref, w_ref, b_ref, o_ref):
    tn = w_ref.shape[1]
    for b in range(cb_ref.shape[0]):
        cb = cb_ref[b]
        act = cb * jax.nn.sigmoid(cb)
        for j in range(tn // LANES):
            sl = slice(j * LANES, (j + 1) * LANES)
            s = jnp.sum(w_ref[:, sl] * act, axis=0, keepdims=True)
            o_ref[b:b + 1, sl] = s + b_ref[:, sl]


def _adaln(c, w_ada, b_ada, tn):
    depth, k, n = w_ada.shape
    nb = c.shape[0]
    cb = jnp.broadcast_to(c[:, :, None], (nb, k, LANES))
    return pl.pallas_call(
        _adaln_kernel,
        grid=(depth, n // tn),
        in_specs=[
            pl.BlockSpec((nb, k, LANES), lambda l, j: (0, 0, 0)),
            pl.BlockSpec((None, k, tn), lambda l, j: (l, 0, j)),
            pl.BlockSpec((None, 1, tn), lambda l, j: (l, 0, j)),
        ],
        out_specs=pl.BlockSpec((None, nb, tn), lambda l, j: (l, 0, j)),
        out_shape=jax.ShapeDtypeStruct((depth, nb, n), F32),
        compiler_params=_cparams(("arbitrary", "arbitrary")),
        name="adaln",
    )(cb, w_ada, b_ada.reshape(depth, 1, n))


def _in_proj_kernel(x_ref, shift_ref, scale_ref, g_ref, w_ref, qg_ref, kg_ref, sgg_ref, o_ref, h_sc, gv_sc,
                    *, head_dim, qk_scale):
    j = pl.program_id(1)

    @pl.when(j == 0)
    def _():
        x = x_ref[...]
        y = x * lax.rsqrt(jnp.mean(x * x, axis=-1, keepdims=True) + EPS)
        h_sc[...] = ((y * g_ref[...]) * (1.0 + scale_ref[...]) + shift_ref[...]).astype(BF16)

    tn = o_ref.shape[1]

    def project(epilogue):
        for c in range(tn // MXU_N):
            sl = slice(c * MXU_N, (c + 1) * MXU_N)
            y = jnp.dot(h_sc[...], w_ref[:, sl].astype(BF16), preferred_element_type=F32)
            epilogue(y, c * MXU_N)

    def head_norm(gain):
        def epilogue(y, c0):
            for h in range(MXU_N // head_dim):
                yh = y[:, h * head_dim:(h + 1) * head_dim]
                r = lax.rsqrt(jnp.mean(yh * yh, axis=-1, keepdims=True) + EPS)
                o_ref[:, c0 + h * head_dim:c0 + (h + 1) * head_dim] = ((yh * r) * gain).astype(BF16)
        return epilogue

    def elementwise(fn):
        def epilogue(y, c0):
            o_ref[:, c0:c0 + MXU_N] = fn(y).astype(BF16)
        return epilogue

    @pl.when(j == 0)
    def _():
        project(head_norm(qg_ref[...] * qk_scale))

    @pl.when(j == 1)
    def _():
        project(head_norm(kg_ref[...]))

    @pl.when(j == 2)
    def _():
        project(elementwise(lambda y: y))

    @pl.when(j == 3)
    def _():
        project(elementwise(jax.nn.gelu))

    @pl.when(j == 4)
    def _():
        ssq = []

        def epilogue(y, c0):
            gv = jax.nn.gelu(y)
            gv_sc[:, c0:c0 + MXU_N] = gv
            ssq.append(jnp.sum(gv * gv, axis=-1, keepdims=True))

        project(epilogue)
        r = lax.rsqrt(sum(ssq) / tn + EPS)
        o_ref[...] = ((gv_sc[...] * r) * sgg_ref[...]).astype(BF16)

    @pl.when(j >= 5)
    def _():
        project(elementwise(jax.nn.sigmoid))


def _in_proj(x2, mod_l, norm_g, w_in, layer, q_g, k_g, sg_g, *, seq, tm, width, head_dim):
    t, d = x2.shape
    n = w_in.shape[2]
    tpb = seq // tm
    vec = lambda which: pl.BlockSpec((None, None, 1, d), lambda i, j: (i // tpb, which, 0, 0))
    const = lambda shape: pl.BlockSpec(shape, lambda i, j: (0,) * len(shape))
    kern = functools.partial(_in_proj_kernel, head_dim=head_dim, qk_scale=float(head_dim) ** -0.5 * LOG2E)
    return pl.pallas_call(
        kern,
        grid=(t // tm, n // width),
        in_specs=[
            pl.BlockSpec((tm, d), lambda i, j: (i, 0), pipeline_mode=pl.Buffered(1)),
            vec(0), vec(1),
            const((1, d)),
            pl.BlockSpec((None, d, width), lambda i, j: (layer, 0, j)),
            const((1, head_dim)), const((1, head_dim)), const((1, width)),
        ],
        out_specs=pl.BlockSpec((tm, width), lambda i, j: (i, j)),
        out_shape=jax.ShapeDtypeStruct((t, n), BF16),
        scratch_shapes=[pltpu.VMEM((tm, d), BF16), pltpu.VMEM((tm, width), F32)],
        compiler_params=_cparams(("arbitrary", "arbitrary")),
        name="in_proj",
    )(x2, mod_l, mod_l, norm_g.reshape(1, d), w_in, q_g.reshape(1, head_dim), k_g.reshape(1, head_dim),
      sg_g.reshape(1, width))


def _sb_attn_kernel(q_ref, k_ref, v_ref, o_ref, carry_sc, acc_sc, *, unroll):
    i = pl.program_id(2)
    tq, dh = q_ref.shape
    q = q_ref[...]
    row = lax.broadcasted_iota(jnp.int32, (tq, tq), 0)
    col = lax.broadcasted_iota(jnp.int32, (tq, tq), 1)
    suffix = (row >= col).astype(BF16)
    strict = col < row

    def scores(kb, diagonal):
        start = pl.multiple_of(kb * tq, tq)
        z = lax.dot_general(q, k_ref[pl.ds(start, tq), :], (((1,), (1,)), ((), ())),
                            preferred_element_type=F32)
        sp = jnp.maximum(z, 0.0) + jnp.log2(1.0 + jnp.exp2(-jnp.abs(z)))
        if diagonal:
            sp = jnp.where(strict, sp, 0.0)
        incl = jnp.dot(sp.astype(BF16), suffix, preferred_element_type=F32)
        return start, z, incl, diagonal

    def accumulate(parts, carry, acc):
        for start, z, incl, diagonal in parts:
            w = jnp.exp2(z - incl - carry)
            if diagonal:
                w = jnp.where(strict, w, 0.0)
            acc = acc + jnp.dot(w.astype(BF16), v_ref[pl.ds(start, tq), :], preferred_element_type=F32)
            carry = carry + incl[:, 0:1]
        return carry, acc

    rem = i % unroll
    for r in range(unroll):
        @pl.when(rem == r)
        def _(r=r):
            parts = [scores(i, True)] + [scores(i - 1 - u, False) for u in range(r)]
            carry, acc = accumulate(parts, jnp.zeros((tq, 1), F32), jnp.zeros((tq, dh), F32))
            carry_sc[...] = carry
            acc_sc[...] = acc

    def group(s, ca):
        first = i - rem - 1 - s * unroll
        return accumulate([scores(first - u, False) for u in range(unroll)], ca[0], ca[1])

    carry, acc = lax.fori_loop(0, i // unroll, group, (carry_sc[...], acc_sc[...]))
    o_ref[...] = acc.astype(BF16)


def _sb_attention(p, *, batch, seq, n_heads, head_dim, tq):
    t = p.shape[0]
    nq = seq // tq
    return pl.pallas_call(
        functools.partial(_sb_attn_kernel, unroll=ATTN_UNROLL),
        grid=(batch, n_heads, nq),
        in_specs=[
            pl.BlockSpec((tq, head_dim), lambda b, h, i: (b * nq + i, h)),
            pl.BlockSpec((seq, head_dim), lambda b, h, i: (b, n_heads + h)),
            pl.BlockSpec((seq, head_dim), lambda b, h, i: (b, 2 * n_heads + h)),
        ],
        out_specs=pl.BlockSpec((tq, head_dim), lambda b, h, i: (b * nq + i, h)),
        out_shape=jax.ShapeDtypeStruct((t, n_heads * head_dim), BF16),
        scratch_shapes=[pltpu.VMEM((tq, 1), F32), pltpu.VMEM((tq, head_dim), F32)],
        compiler_params=_cparams(("arbitrary", "arbitrary", "arbitrary")),
        name="sb_attention",
    )(p, p, p)


def _route_top2(logits):
    n_e = logits.shape[1]
    lane = lax.broadcasted_iota(jnp.int32, logits.shape, 1)
    m1 = jnp.max(logits, axis=-1, keepdims=True)
    i1 = jnp.min(jnp.where(logits == m1, lane, n_e), axis=-1, keepdims=True)
    rest = jnp.where(lane == i1, -jnp.inf, logits)
    m2 = jnp.max(rest, axis=-1, keepdims=True)
    i2 = jnp.min(jnp.where(rest == m2, lane, n_e), axis=-1, keepdims=True)
    t = jnp.exp(m2 - m1)
    w1 = 1.0 / (1.0 + t)
    w2 = t / (1.0 + t)
    out = jnp.where(lane == 0, i1.astype(F32), 0.0)
    out = jnp.where(lane == 1, i2.astype(F32), out)
    out = jnp.where(lane == 2, w1, out)
    out = jnp.where(lane == 3, w2, out)
    return out


def _merge_kernel(*refs, chunk, gdim, moe, dw):
    ysb_ref, su_ref, sv_ref = refs[:3]
    ga_refs = refs[3:3 + dw]
    gb_refs = refs[3 + dw:3 + 2 * dw]
    refs = refs[3 + 2 * dw:]
    x_ref, gate1_ref, shift2_ref, scale2_ref, g2_ref, ws_ref, bs_ref, pa_ref, pb_ref, wo_ref = refs[:10]
    if moe:
        wr_ref, xo_ref, h_ref, route_ref, ysg_sc, m_sc = refs[10:]
    else:
        xo_ref, h_ref, ysg_sc, m_sc = refs[10:]
    tm = x_ref.shape[0]
    width = ysb_ref.shape[1]
    n_groups = ws_ref.shape[0]
    row = lax.broadcasted_iota(jnp.int32, (chunk, chunk), 0)
    col = lax.broadcasted_iota(jnp.int32, (chunk, chunk), 1)
    causal = col <= row
    for g in range(n_groups):
        cs = slice(g * gdim, (g + 1) * gdim)
        wg = jnp.where(causal, ws_ref[g], 0.0).astype(BF16)
        for ch in range(tm // chunk):
            rs = slice(ch * chunk, (ch + 1) * chunk)
            mixed = jnp.dot(wg, sv_ref[rs, cs], preferred_element_type=F32) + bs_ref[g]
            ysg_sc[rs, cs] = (su_ref[rs, cs].astype(F32) * mixed).astype(BF16)
    for s in range(dw):
        sl = slice(s * width, (s + 1) * width)
        a = jnp.dot(ysb_ref[...], pa_ref[:, sl], preferred_element_type=F32)
        b = jnp.dot(ysg_sc[...], pb_ref[:, sl], preferred_element_type=F32)
        m_sc[:, sl] = (ga_refs[s][...].astype(F32) * a + gb_refs[s][...].astype(F32) * b).astype(BF16)
    o = jnp.dot(m_sc[...], wo_ref[...], preferred_element_type=F32)
    xn = x_ref[...] + gate1_ref[...] * o
    xo_ref[...] = xn
    y = xn * lax.rsqrt(jnp.mean(xn * xn, axis=-1, keepdims=True) + EPS)
    h = (y * g2_ref[...]) * (1.0 + scale2_ref[...]) + shift2_ref[...]
    h_ref[...] = h.astype(h_ref.dtype)
    if moe:
        wr = wr_ref[...]
        h_hi, w_hi = h.astype(BF16), wr.astype(BF16)
        h_lo = (h - h_hi.astype(F32)).astype(BF16)
        w_lo = (wr - w_hi.astype(F32)).astype(BF16)
        logits = (jnp.dot(h_hi, w_hi, preferred_element_type=F32)
                  + jnp.dot(h_lo, w_hi, preferred_element_type=F32)
                  + jnp.dot(h_hi, w_lo, preferred_element_type=F32))
        route_ref[...] = _route_top2(logits)


def _merge(ysb, p, x2, mod_l, norm2_g, w_s, bs_b, pa_bf, pb_bf, wo_bf, w_router, *, seq, tm, width):
    t, d = x2.shape
    n_groups, chunk, _ = w_s.shape
    gdim = width // n_groups
    tpb = seq // tm
    moe = w_router is not None
    dw = d // width
    vec = lambda which: pl.BlockSpec((None, None, 1, d), lambda i: (i // tpb, which, 0, 0))
    resident = lambda shape: pl.BlockSpec(shape, lambda i: (0,) * len(shape), pipeline_mode=pl.Buffered(1))
    pcol = lambda cb: pl.BlockSpec((tm, width), lambda i: (i, cb))
    in_specs = [
        pl.BlockSpec((tm, width), lambda i: (i, 0)),
        pcol(3),
        pcol(4),
        *[pcol(5 + s) for s in range(dw)],
        *[pcol(5 + dw + s) for s in range(dw)],
        pl.BlockSpec((tm, d), lambda i: (i, 0)),
        vec(2), vec(3), vec(4),
        resident((1, d)),
        resident((n_groups, chunk, chunk)),
        resident((n_groups, chunk, gdim)),
        resident((width, d)), resident((width, d)), resident((d, d)),
    ]
    args = [ysb, p, p, *([p] * (2 * dw)), x2, mod_l, mod_l, mod_l, norm2_g.reshape(1, d), w_s, bs_b,
            pa_bf, pb_bf, wo_bf]
    out_specs = [pl.BlockSpec((tm, d), lambda i: (i, 0)), pl.BlockSpec((tm, d), lambda i: (i, 0))]
    out_shape = [jax.ShapeDtypeStruct((t, d), F32), jax.ShapeDtypeStruct((t, d), F32 if moe else BF16)]
    if moe:
        n_e = w_router.shape[1]
        in_specs.append(resident((d, n_e)))
        args.append(w_router)
        out_specs.append(pl.BlockSpec((tm, n_e), lambda i: (i, 0)))
        out_shape.append(jax.ShapeDtypeStruct((t, n_e), F32))
    kern = functools.partial(_merge_kernel, chunk=chunk, gdim=gdim, moe=moe, dw=dw)
    return pl.pallas_call(
        kern,
        grid=(t // tm,),
        in_specs=in_specs,
        out_specs=out_specs,
        out_shape=out_shape,
        scratch_shapes=[pltpu.VMEM((tm, width), BF16), pltpu.VMEM((tm, d), BF16)],
        compiler_params=_cparams(("arbitrary",)),
        name="merge_moe" if moe else "merge_dense",
    )(*args)


def _swiglu_step(x_ref, w_refs, w_scs, o_ref, n_sub, sub):
    for w_ref, w_sc in zip(w_refs, w_scs):
        w_sc[...] = w_ref[...].astype(BF16)
    w1_sc, w3_sc, w2_sc = w_scs

    def rows(s):
        r0 = pl.multiple_of(s * sub, sub)
        xs = x_ref[pl.ds(r0, sub), :]
        a = jnp.dot(xs, w1_sc[...], preferred_element_type=F32)
        b = jnp.dot(xs, w3_sc[...], preferred_element_type=F32)
        hh = (jax.nn.silu(a) * b).astype(BF16)
        o_ref[pl.ds(r0, sub), :] += jnp.dot(hh, w2_sc[...], preferred_element_type=F32)

    def pair(s, c):
        rows(2 * s)
        rows(2 * s + 1)
        return c

    lax.fori_loop(0, n_sub // 2, pair, 0)
    if isinstance(n_sub, int):
        if n_sub % 2:
            rows(n_sub - 1)
    else:
        @pl.when(n_sub % 2 == 1)
        def _():
            rows(n_sub - 1)


def _ffn_dense_kernel(x_ref, w1_ref, w3_ref, w2_ref, res_ref, gate_ref, o_ref, w1_sc, w3_sc, w2_sc, *, sub):
    j = pl.program_id(1)

    @pl.when(j == 0)
    def _():
        o_ref[...] = jnp.zeros_like(o_ref)

    _swiglu_step(x_ref, (w1_ref, w3_ref, w2_ref), (w1_sc, w3_sc, w2_sc), o_ref, x_ref.shape[0] // sub, sub)

    @pl.when(j == pl.num_programs(1) - 1)
    def _():
        o_ref[...] = res_ref[...] + gate_ref[...] * o_ref[...]


def _ffn_dense(h, w1, w3, w2, res, mod_l, *, seq, tm, tf, sub):
    n, d = h.shape
    n_f = w1.shape[1]
    tpb = seq // tm
    return pl.pallas_call(
        functools.partial(_ffn_dense_kernel, sub=sub),
        grid=(n // tm, n_f // tf),
        in_specs=[
            pl.BlockSpec((tm, d), lambda i, j: (i, 0)),
            pl.BlockSpec((d, tf), lambda i, j: (0, j)),
            pl.BlockSpec((d, tf), lambda i, j: (0, j)),
            pl.BlockSpec((tf, d), lambda i, j: (j, 0)),
            pl.BlockSpec((tm, d), lambda i, j: (i, 0), pipeline_mode=pl.Buffered(1)),
            pl.BlockSpec((None, None, 1, d), lambda i, j: (i // tpb, 5, 0, 0)),
        ],
        out_specs=pl.BlockSpec((tm, d), lambda i, j: (i, 0)),
        out_shape=jax.ShapeDtypeStruct((n, d), F32),
        scratch_shapes=[pltpu.VMEM((d, tf), BF16), pltpu.VMEM((d, tf), BF16), pltpu.VMEM((tf, d), BF16)],
        compiler_params=_cparams(("arbitrary", "arbitrary")),
        name="ffn_dense",
    )(h, w1, w3, w2, res, mod_l)


def _row_copy(src_hbm, src_row, dst_ref, dst_row, sem):
    return pltpu.make_async_copy(src_hbm.at[pl.ds(src_row, 1), :], dst_ref.at[pl.ds(dst_row, 1), :], sem)


def _ffn_moe_kernel(te_ref, tv_ref, src_ref, x_hbm, w1_ref, w3_ref, w2_ref, o_ref, xg_sc, xb_sc, w1_sc, w3_sc,
                    w2_sc, sem, *, sub, rows_per_step):
    del te_ref
    i = pl.program_id(0)
    j = pl.program_id(1)
    tm = xb_sc.shape[0]
    n_sub = tv_ref[i]

    def gather_rows(tile, lo, hi):
        slot = tile % 2

        def start(r, c):
            _row_copy(x_hbm, src_ref[tile * tm + r], xg_sc.at[slot], r, sem.at[slot]).start()
            return c

        lax.fori_loop(lo, hi, start, 0)

    @pl.when(jnp.logical_and(i == 0, j == 0))
    def _():
        gather_rows(0, 0, tv_ref[0] * sub)

    @pl.when(i + 1 < pl.num_programs(0))
    def _():
        n_next = tv_ref[i + 1] * sub
        lo = jnp.minimum(j * rows_per_step, n_next)
        gather_rows(i + 1, lo, jnp.minimum(lo + rows_per_step, n_next))

    @pl.when(j == 0)
    def _():
        slot = i % 2

        def wait(r, c):
            _row_copy(x_hbm, 0, xg_sc.at[slot], r, sem.at[slot]).wait()
            return c

        lax.fori_loop(0, n_sub * sub, wait, 0)

        def cast(s, c):
            r0 = pl.multiple_of(s * sub, sub)
            xb_sc[pl.ds(r0, sub), :] = xg_sc[slot, pl.ds(r0, sub), :].astype(BF16)
            return c

        lax.fori_loop(0, n_sub, cast, 0)
        o_ref[...] = jnp.zeros_like(o_ref)

    @pl.when(n_sub > 0)
    def _():
        _swiglu_step(xb_sc, (w1_ref, w3_ref, w2_ref), (w1_sc, w3_sc, w2_sc), o_ref, n_sub, sub)


def _ffn_moe(h, w1, w3, w2, tile_e, tile_v, src, *, tm, tf, sub):
    n_pad = src.shape[0]
    d = h.shape[1]
    n_f = w1.shape[2]
    nj = n_f // tf
    jj = lambda i, j, tv: jnp.where(tv[i] > 0, j, nj - 1)
    return pl.pallas_call(
        functools.partial(_ffn_moe_kernel, sub=sub, rows_per_step=-(-tm // nj)),
        grid_spec=pltpu.PrefetchScalarGridSpec(
            num_scalar_prefetch=3,
            grid=(n_pad // tm, nj),
            in_specs=[
                pl.BlockSpec(memory_space=pl.ANY),
                pl.BlockSpec((None, d, tf), lambda i, j, te, tv, src: (te[i], 0, jj(i, j, tv))),
                pl.BlockSpec((None, d, tf), lambda i, j, te, tv, src: (te[i], 0, jj(i, j, tv))),
                pl.BlockSpec((None, tf, d), lambda i, j, te, tv, src: (te[i], jj(i, j, tv), 0)),
            ],
            out_specs=pl.BlockSpec((tm, d), lambda i, j, te, tv, src: (i, 0)),
            scratch_shapes=[pltpu.VMEM((2, tm, d), h.dtype), pltpu.VMEM((tm, d), BF16),
                            pltpu.VMEM((d, tf), BF16), pltpu.VMEM((d, tf), BF16), pltpu.VMEM((tf, d), BF16),
                            pltpu.SemaphoreType.DMA((2,))],
        ),
        out_shape=jax.ShapeDtypeStruct((n_pad, d), F32),
        compiler_params=_cparams(("arbitrary", "arbitrary")),
        name="ffn_moe",
    )(tile_e, tile_v, src, h, w1, w3, w2)


def _combine_kernel(pos0_ref, pos1_ref, ys_hbm, x_ref, gate_ref, w0_ref, w1_ref, o_ref, buf0, buf1, sem):
    tc = x_ref.shape[0]
    base = pl.program_id(0) * tc

    def issue(r, c):
        _row_copy(ys_hbm, pos0_ref[base + r], buf0, r, sem).start()
        _row_copy(ys_hbm, pos1_ref[base + r], buf1, r, sem).start()
        return c

    lax.fori_loop(0, tc, issue, 0)

    def wait(r, c):
        _row_copy(ys_hbm, 0, buf0, r, sem).wait()
        _row_copy(ys_hbm, 0, buf1, r, sem).wait()
        return c

    lax.fori_loop(0, tc, wait, 0)
    o_ref[...] = x_ref[...] + gate_ref[...] * (w0_ref[...] * buf0[...] + w1_ref[...] * buf1[...])


def _combine(pos0, pos1, ys, x2, mod_l, w0, w1, *, seq, tc):
    t, d = x2.shape
    tpb = seq // tc
    return pl.pallas_call(
        _combine_kernel,
        grid_spec=pltpu.PrefetchScalarGridSpec(
            num_scalar_prefetch=2,
            grid=(t // tc,),
            in_specs=[pl.BlockSpec(memory_space=pl.ANY),
                      pl.BlockSpec((tc, d), lambda i, p0, p1: (i, 0)),
                      pl.BlockSpec((None, None, 1, d), lambda i, p0, p1: (i // tpb, 5, 0, 0)),
                      pl.BlockSpec((tc, 1), lambda i, p0, p1: (i, 0)),
                      pl.BlockSpec((tc, 1), lambda i, p0, p1: (i, 0))],
            out_specs=pl.BlockSpec((tc, d), lambda i, p0, p1: (i, 0)),
            scratch_shapes=[pltpu.VMEM((tc, d), F32), pltpu.VMEM((tc, d), F32), pltpu.SemaphoreType.DMA(())],
        ),
        out_shape=jax.ShapeDtypeStruct((t, d), F32),
        compiler_params=_cparams(("arbitrary",)),
        name="moe_combine",
    )(pos0, pos1, ys, x2, mod_l, w0, w1)


def _route_metadata(route, n_experts, tm, sub):
    t = route.shape[0]
    e = route[:, :TOP_K].astype(jnp.int32)
    w = route[:, TOP_K:2 * TOP_K]
    cnt = jnp.sum((e[:, :, None] == jnp.arange(n_experts, dtype=jnp.int32)).astype(jnp.int32), axis=1)
    csum = jnp.cumsum(cnt, axis=0)
    rank = csum - cnt
    counts = csum[-1]
    padded = ((counts + tm - 1) // tm) * tm
    ends = jnp.cumsum(padded)
    starts = ends - padded
    pos = starts[e] + jnp.take_along_axis(rank, e, axis=1)
    n_tiles = (TOP_K * t + n_experts * (tm - 1)) // tm
    n_pad = n_tiles * tm
    tok = jnp.broadcast_to(jnp.arange(t, dtype=jnp.int32)[:, None], (t, TOP_K))
    flat = pos.reshape(-1)
    src = jnp.zeros((n_pad,), jnp.int32).at[flat].set(tok.reshape(-1), unique_indices=True)
    tile_start = jnp.arange(n_tiles, dtype=jnp.int32) * tm
    tile_e = jnp.sum((tile_start[:, None] >= ends[None, :]).astype(jnp.int32), axis=1)
    tile_e = jnp.minimum(tile_e, n_experts - 1)
    real_rows = jnp.clip(starts[tile_e] + counts[tile_e] - tile_start, 0, tm)
    tile_v = ((real_rows + sub - 1) // sub).astype(jnp.int32)
    return pos[:, 0], pos[:, 1], w[:, 0:1], w[:, 1:2], src, tile_e, tile_v


def kernel(x, c, w_ada, b_ada, norm1_g, norm2_g, w_in, q_norm_g, k_norm_g, sg_norm_g, w_s, b_s, w_proj_sb,
           w_proj_sg, w_out, ffn_w1, ffn_w3, ffn_w2, moe_router, moe_w1, moe_w3, moe_w2):
    batch, seq, d = x.shape
    depth = w_ada.shape[0]
    head_dim = q_norm_g.shape[1]
    width = w_proj_sb.shape[1]
    n_heads = width // head_dim
    n_groups, chunk = w_s.shape[1], w_s.shape[2]
    gdim = width // n_groups
    n_experts = moe_router.shape[2]
    assert sg_norm_g.shape[1] == width and d % width == 0 and w_in.shape[2] == 5 * width + 2 * d
    assert head_dim % LANES == 0 and gdim % LANES == 0 and n_experts >= 2 * TOP_K
    cfg = _tiles(seq, d)
    t = batch * seq

    mod = _adaln(c, w_ada, b_ada, cfg["ada_tn"]).reshape(depth, batch, N_MOD, 1, d)
    x2 = x.reshape(t, d)
    for i in range(depth):
        mod_l = mod[i]
        p = _in_proj(x2, mod_l, norm1_g[i], w_in, i, q_norm_g[i], k_norm_g[i], sg_norm_g[i],
                     seq=seq, tm=cfg["in_tm"], width=width, head_dim=head_dim)
        ysb = _sb_attention(p, batch=batch, seq=seq, n_heads=n_heads, head_dim=head_dim, tq=cfg["attn_tq"])
        bs_b = jnp.broadcast_to(b_s[i][:, :, None], (n_groups, chunk, gdim))
        moe = i % 2 == 1
        outs = _merge(ysb, p, x2, mod_l, norm2_g[i], w_s[i], bs_b, w_proj_sb[i].astype(BF16),
                      w_proj_sg[i].astype(BF16), w_out[i].astype(BF16), moe_router[i // 2] if moe else None,
                      seq=seq, tm=cfg["merge_tm"], width=width)
        if not moe:
            x2, h = outs
            jd = i // 2
            x2 = _ffn_dense(h, ffn_w1[jd], ffn_w3[jd], ffn_w2[jd], x2, mod_l, seq=seq,
                            tm=cfg["ffn_tm"], tf=cfg["ffn_tf"], sub=cfg["ffn_sub"])
        else:
            x2, h, route = outs
            jm = i // 2
            pos0, pos1, w0, w1, src, tile_e, tile_v = _route_metadata(route, n_experts, cfg["ffn_tm"],
                                                                      cfg["ffn_sub"])
            ys = _ffn_moe(h, moe_w1[jm], moe_w3[jm], moe_w2[jm], tile_e, tile_v, src,
                          tm=cfg["ffn_tm"], tf=cfg["ffn_tf"], sub=cfg["ffn_sub"])
            x2 = _combine(pos0, pos1, ys, x2, mod_l, w0, w1, seq=seq, tc=cfg["gather_rows"])
    return x2.reshape(batch, seq, d)
```

```python
import functools

import jax
import jax.numpy as jnp
from jax import lax
from jax.experimental import pallas as pl
from jax.experimental.pallas import tpu as pltpu

F32 = jnp.float32
BF16 = jnp.bfloat16
EPS = 1e-6
N_MOD = 6
TOP_K = 2
LOG2E = 1.4426950408889634
ATTN_UNROLL = 8
LANES = 128
MXU_N = 256
DMA_GROUP = 8
VMEM_LIMIT = 56 * 1024 * 1024


def _tiles(seq, d_model):
    return dict(
        ada_tn=min(1024, d_model),
        in_tm=min(1024, seq),
        attn_tq=min(256, seq),
        merge_tm=min(256, seq),
        ffn_tm=min(1024, seq),
        ffn_unit=min(128, seq // 4),
        moe_tm=min(1024, seq) + min(128, seq // 4),
        ffn_tf=256,
        gather_rows=min(256, seq),
    )


def _cparams(sem):
    return pltpu.CompilerParams(dimension_semantics=sem, vmem_limit_bytes=VMEM_LIMIT)


def _adaln_kernel(cb_ref, w_ref, b_ref, o_ref):
    tn = w_ref.shape[1]
    for b in range(cb_ref.shape[0]):
        cb = cb_ref[b]
        act = cb * jax.nn.sigmoid(cb)
        for j in range(tn // LANES):
            sl = slice(j * LANES, (j + 1) * LANES)
            s = jnp.sum(w_ref[:, sl] * act, axis=0, keepdims=True)
            o_ref[b:b + 1, sl] = s + b_ref[:, sl]


def _adaln(c, w_ada, b_ada, tn):
    depth, k, n = w_ada.shape
    nb = c.shape[0]
    cb = jnp.broadcast_to(c[:, :, None], (nb, k, LANES))
    return pl.pallas_call(
        _adaln_kernel,
        grid=(depth, n // tn),
        in_specs=[
            pl.BlockSpec((nb, k, LANES), lambda l, j: (0, 0, 0)),
            pl.BlockSpec((None, k, tn), lambda l, j: (l, 0, j)),
            pl.BlockSpec((None, 1, tn), lambda l, j: (l, 0, j)),
        ],
        out_specs=pl.BlockSpec((None, nb, tn), lambda l, j: (l, 0, j)),
        out_shape=jax.ShapeDtypeStruct((depth, nb, n), F32),
        compiler_params=_cparams(("arbitrary", "arbitrary")),
        name="adaln",
    )(cb, w_ada, b_ada.reshape(depth, 1, n))


def _in_proj_kernel(x_ref, shift_ref, scale_ref, g_ref, w_ref, qg_ref, kg_ref, sgg_ref, o_ref, h_sc, gv_sc,
                    *, head_dim, qk_scale):
    j = pl.program_id(1)

    @pl.when(j == 0)
    def _():
        x = x_ref[...]
        y = x * lax.rsqrt(jnp.mean(x * x, axis=-1, keepdims=True) + EPS)
        h_sc[...] = ((y * g_ref[...]) * (1.0 + scale_ref[...]) + shift_ref[...]).astype(BF16)

    tn = o_ref.shape[1]

    def project(epilogue):
        for c in range(tn // MXU_N):
            sl = slice(c * MXU_N, (c + 1) * MXU_N)
            y = jnp.dot(h_sc[...], w_ref[:, sl].astype(BF16), preferred_element_type=F32)
            epilogue(y, c * MXU_N)

    def head_norm(gain):
        def epilogue(y, c0):
            for h in range(MXU_N // head_dim):
                yh = y[:, h * head_dim:(h + 1) * head_dim]
                r = lax.rsqrt(jnp.mean(yh * yh, axis=-1, keepdims=True) + EPS)
                o_ref[:, c0 + h * head_dim:c0 + (h + 1) * head_dim] = ((yh * r) * gain).astype(BF16)
        return epilogue

    def elementwise(fn):
        def epilogue(y, c0):
            o_ref[:, c0:c0 + MXU_N] = fn(y).astype(BF16)
        return epilogue

    @pl.when(j == 0)
    def _():
        project(head_norm(qg_ref[...] * qk_scale))

    @pl.when(j == 1)
    def _():
        project(head_norm(kg_ref[...]))

    @pl.when(j == 2)
    def _():
        project(elementwise(lambda y: y))

    @pl.when(j == 3)
    def _():
        project(elementwise(jax.nn.gelu))

    @pl.when(j == 4)
    def _():
        ssq = []

        def epilogue(y, c0):
            gv = jax.nn.gelu(y)
            gv_sc[:, c0:c0 + MXU_N] = gv
            ssq.append(jnp.sum(gv * gv, axis=-1, keepdims=True))

        project(epilogue)
        r = lax.rsqrt(sum(ssq) / tn + EPS)
        o_ref[...] = ((gv_sc[...] * r) * sgg_ref[...]).astype(BF16)

    @pl.when(j >= 5)
    def _():
        project(elementwise(jax.nn.sigmoid))


def _in_proj(x2, mod_l, norm_g, w_in, layer, q_g, k_g, sg_g, *, seq, tm, width, head_dim):
    t, d = x2.shape
    n = w_in.shape[2]
    tpb = seq // tm
    vec = lambda which: pl.BlockSpec((None, None, 1, d), lambda i, j: (i // tpb, which, 0, 0))
    const = lambda shape: pl.BlockSpec(shape, lambda i, j: (0,) * len(shape))
    kern = functools.partial(_in_proj_kernel, head_dim=head_dim, qk_scale=float(head_dim) ** -0.5 * LOG2E)
    return pl.pallas_call(
        kern,
        grid=(t // tm, n // width),
        in_specs=[
            pl.BlockSpec((tm, d), lambda i, j: (i, 0), pipeline_mode=pl.Buffered(1)),
            vec(0), vec(1),
            const((1, d)),
            pl.BlockSpec((None, d, width), lambda i, j: (layer, 0, j)),
            const((1, head_dim)), const((1, head_dim)), const((1, width)),
        ],
        out_specs=pl.BlockSpec((tm, width), lambda i, j: (i, j)),
        out_shape=jax.ShapeDtypeStruct((t, n), BF16),
        scratch_shapes=[pltpu.VMEM((tm, d), BF16), pltpu.VMEM((tm, width), F32)],
        compiler_params=_cparams(("arbitrary", "arbitrary")),
        name="in_proj",
    )(x2, mod_l, mod_l, norm_g.reshape(1, d), w_in, q_g.reshape(1, head_dim), k_g.reshape(1, head_dim),
      sg_g.reshape(1, width))


def _sb_attn_kernel(q_ref, k_ref, v_ref, o_ref, carry_sc, acc_sc, *, unroll):
    i = pl.program_id(2)
    tq, dh = q_ref.shape
    q = q_ref[...]
    row = lax.broadcasted_iota(jnp.int32, (tq, tq), 0)
    col = lax.broadcasted_iota(jnp.int32, (tq, tq), 1)
    suffix = (row >= col).astype(BF16)
    strict = col < row

    def scores(kb, diagonal):
        start = pl.multiple_of(kb * tq, tq)
        z = lax.dot_general(q, k_ref[pl.ds(start, tq), :], (((1,), (1,)), ((), ())),
                            preferred_element_type=F32)
        sp = jnp.maximum(z, 0.0) + jnp.log2(1.0 + jnp.exp2(-jnp.abs(z)))
        if diagonal:
            sp = jnp.where(strict, sp, 0.0)
        incl = jnp.dot(sp.astype(BF16), suffix, preferred_element_type=F32)
        return start, z, incl, diagonal

    def accumulate(parts, carry, acc):
        for start, z, incl, diagonal in parts:
            w = jnp.exp2(z - incl - carry)
            if diagonal:
                w = jnp.where(strict, w, 0.0)
            acc = acc + jnp.dot(w.astype(BF16), v_ref[pl.ds(start, tq), :], preferred_element_type=F32)
            carry = carry + incl[:, 0:1]
        return carry, acc

    rem = i % unroll
    for r in range(unroll):
        @pl.when(rem == r)
        def _(r=r):
            parts = [scores(i, True)] + [scores(i - 1 - u, False) for u in range(r)]
            carry, acc = accumulate(parts, jnp.zeros((tq, 1), F32), jnp.zeros((tq, dh), F32))
            carry_sc[...] = carry
            acc_sc[...] = acc

    def group(s, ca):
        first = i - rem - 1 - s * unroll
        return accumulate([scores(first - u, False) for u in range(unroll)], ca[0], ca[1])

    carry, acc = lax.fori_loop(0, i // unroll, group, (carry_sc[...], acc_sc[...]))
    o_ref[...] = acc.astype(BF16)


def _sb_attention(p, *, batch, seq, n_heads, head_dim, tq):
    t = p.shape[0]
    nq = seq // tq
    return pl.pallas_call(
        functools.partial(_sb_attn_kernel, unroll=ATTN_UNROLL),
        grid=(batch, n_heads, nq),
        in_specs=[
            pl.BlockSpec((tq, head_dim), lambda b, h, i: (b * nq + i, h)),
            pl.BlockSpec((seq, head_dim), lambda b, h, i: (b, n_heads + h)),
            pl.BlockSpec((seq, head_dim), lambda b, h, i: (b, 2 * n_heads + h)),
        ],
        out_specs=pl.BlockSpec((tq, head_dim), lambda b, h, i: (b * nq + i, h)),
        out_shape=jax.ShapeDtypeStruct((t, n_heads * head_dim), BF16),
        scratch_shapes=[pltpu.VMEM((tq, 1), F32), pltpu.VMEM((tq, head_dim), F32)],
        compiler_params=_cparams(("arbitrary", "arbitrary", "arbitrary")),
        name="sb_attention",
    )(p, p, p)


def _route_top2(logits):
    n_e = logits.shape[1]
    lane = lax.broadcasted_iota(jnp.int32, logits.shape, 1)
    m1 = jnp.max(logits, axis=-1, keepdims=True)
    i1 = jnp.min(jnp.where(logits == m1, lane, n_e), axis=-1, keepdims=True)
    rest = jnp.where(lane == i1, -jnp.inf, logits)
    m2 = jnp.max(rest, axis=-1, keepdims=True)
    i2 = jnp.min(jnp.where(rest == m2, lane, n_e), axis=-1, keepdims=True)
    t = jnp.exp(m2 - m1)
    w1 = 1.0 / (1.0 + t)
    w2 = t / (1.0 + t)
    out = jnp.where(lane == 0, i1.astype(F32), 0.0)
    out = jnp.where(lane == 1, i2.astype(F32), out)
    out = jnp.where(lane == 2, w1, out)
    out = jnp.where(lane == 3, w2, out)
    return out


def _merge_kernel(*refs, chunk, gdim, moe, dw):
    ysb_ref, su_ref, sv_ref = refs[:3]
    ga_refs = refs[3:3 + dw]
    gb_refs = refs[3 + dw:3 + 2 * dw]
    refs = refs[3 + 2 * dw:]
    x_ref, gate1_ref, shift2_ref, scale2_ref, g2_ref, ws_ref, bs_ref, pa_ref, pb_ref, wo_ref = refs[:10]
    if moe:
        wr_ref, xo_ref, h_ref, route_ref, ysg_sc, m_sc = refs[10:]
    else:
        xo_ref, h_ref, ysg_sc, m_sc = refs[10:]
    tm = x_ref.shape[0]
    width = ysb_ref.shape[1]
    n_groups = ws_ref.shape[0]
    row = lax.broadcasted_iota(jnp.int32, (chunk, chunk), 0)
    col = lax.broadcasted_iota(jnp.int32, (chunk, chunk), 1)
    causal = col <= row
    for g in range(n_groups):
        cs = slice(g * gdim, (g + 1) * gdim)
        wg = jnp.where(causal, ws_ref[g], 0.0).astype(BF16)
        for ch in range(tm // chunk):
            rs = slice(ch * chunk, (ch + 1) * chunk)
            mixed = jnp.dot(wg, sv_ref[rs, cs], preferred_element_type=F32) + bs_ref[g]
            ysg_sc[rs, cs] = (su_ref[rs, cs].astype(F32) * mixed).astype(BF16)
    for s in range(dw):
        sl = slice(s * width, (s + 1) * width)
        a = jnp.dot(ysb_ref[...], pa_ref[:, sl], preferred_element_type=F32)
        b = jnp.dot(ysg_sc[...], pb_ref[:, sl], preferred_element_type=F32)
        m_sc[:, sl] = (ga_refs[s][...].astype(F32) * a + gb_refs[s][...].astype(F32) * b).astype(BF16)
    o = jnp.dot(m_sc[...], wo_ref[...], preferred_element_type=F32)
    xn = x_ref[...] + gate1_ref[...] * o
    xo_ref[...] = xn
    y = xn * lax.rsqrt(jnp.mean(xn * xn, axis=-1, keepdims=True) + EPS)
    h = (y * g2_ref[...]) * (1.0 + scale2_ref[...]) + shift2_ref[...]
    h_ref[...] = h.astype(h_ref.dtype)
    if moe:
        wr = wr_ref[...]
        h_hi, w_hi = h.astype(BF16), wr.astype(BF16)
        h_lo = (h - h_hi.astype(F32)).astype(BF16)
        w_lo = (wr - w_hi.astype(F32)).astype(BF16)
        logits = (jnp.dot(h_hi, w_hi, preferred_element_type=F32)
                  + jnp.dot(h_lo, w_hi, preferred_element_type=F32)
                  + jnp.dot(h_hi, w_lo, preferred_element_type=F32))
        route_ref[...] = _route_top2(logits)


def _merge(ysb, p, x2, mod_l, norm2_g, w_s, bs_b, pa_bf, pb_bf, wo_bf, w_router, *, seq, tm, width):
    t, d = x2.shape
    n_groups, chunk, _ = w_s.shape
    gdim = width // n_groups
    tpb = seq // tm
    moe = w_router is not None
    dw = d // width
    vec = lambda which: pl.BlockSpec((None, None, 1, d), lambda i: (i // tpb, which, 0, 0))
    resident = lambda shape: pl.BlockSpec(shape, lambda i: (0,) * len(shape), pipeline_mode=pl.Buffered(1))
    pcol = lambda cb: pl.BlockSpec((tm, width), lambda i: (i, cb))
    in_specs = [
        pl.BlockSpec((tm, width), lambda i: (i, 0)),
        pcol(3),
        pcol(4),
        *[pcol(5 + s) for s in range(dw)],
        *[pcol(5 + dw + s) for s in range(dw)],
        pl.BlockSpec((tm, d), lambda i: (i, 0)),
        vec(2), vec(3), vec(4),
        resident((1, d)),
        resident((n_groups, chunk, chunk)),
        resident((n_groups, chunk, gdim)),
        resident((width, d)), resident((width, d)), resident((d, d)),
    ]
    args = [ysb, p, p, *([p] * (2 * dw)), x2, mod_l, mod_l, mod_l, norm2_g.reshape(1, d), w_s, bs_b,
            pa_bf, pb_bf, wo_bf]
    out_specs = [pl.BlockSpec((tm, d), lambda i: (i, 0)), pl.BlockSpec((tm, d), lambda i: (i, 0))]
    out_shape = [jax.ShapeDtypeStruct((t, d), F32), jax.ShapeDtypeStruct((t, d), F32 if moe else BF16)]
    if moe:
        n_e = w_router.shape[1]
        in_specs.append(resident((d, n_e)))
        args.append(w_router)
        out_specs.append(pl.BlockSpec((tm, n_e), lambda i: (i, 0)))
        out_shape.append(jax.ShapeDtypeStruct((t, n_e), F32))
    kern = functools.partial(_merge_kernel, chunk=chunk, gdim=gdim, moe=moe, dw=dw)
    return pl.pallas_call(
        kern,
        grid=(t // tm,),
        in_specs=in_specs,
        out_specs=out_specs,
        out_shape=out_shape,
        scratch_shapes=[pltpu.VMEM((tm, width), BF16), pltpu.VMEM((tm, d), BF16)],
        compiler_params=_cparams(("arbitrary",)),
        name="merge_moe" if moe else "merge_dense",
    )(*args)


def _swiglu_step(x_ref, w_refs, w_scs, o_ref, n_units, unit):
    for w_ref, w_sc in zip(w_refs, w_scs):
        w_sc[...] = w_ref[...].astype(BF16)
    w1_sc, w3_sc, w2_sc = w_scs
    big = 4 * unit

    def rows(r0, m):
        xs = x_ref[pl.ds(r0, m), :]
        a = jnp.dot(xs, w1_sc[...], preferred_element_type=F32)
        b = jnp.dot(xs, w3_sc[...], preferred_element_type=F32)
        hh = (jax.nn.silu(a) * b).astype(BF16)
        o_ref[pl.ds(r0, m), :] += jnp.dot(hh, w2_sc[...], preferred_element_type=F32)

    if isinstance(n_units, int):
        assert n_units % 4 == 0
        for s in range(n_units // 4):
            rows(s * big, big)
        return

    n_big = n_units // 4
    rem = n_units % 4

    def pair(s, c):
        rows(pl.multiple_of(2 * s * big, big), big)
        rows(pl.multiple_of(2 * s * big + big, big), big)
        return c

    lax.fori_loop(0, n_big // 2, pair, 0)

    @pl.when(n_big % 2 == 1)
    def _():
        rows(pl.multiple_of((n_big - 1) * big, big), big)

    @pl.when(rem >= 2)
    def _():
        rows(pl.multiple_of(n_big * big, unit), 2 * unit)

    @pl.when(rem % 2 == 1)
    def _():
        rows(pl.multiple_of(n_big * big + (rem // 2) * 2 * unit, unit), unit)


def _ffn_dense_kernel(x_ref, w1_ref, w3_ref, w2_ref, res_ref, gate_ref, o_ref, w1_sc, w3_sc, w2_sc, *, unit):
    j = pl.program_id(1)

    @pl.when(j == 0)
    def _():
        o_ref[...] = jnp.zeros_like(o_ref)

    _swiglu_step(x_ref, (w1_ref, w3_ref, w2_ref), (w1_sc, w3_sc, w2_sc), o_ref, x_ref.shape[0] // unit, unit)

    @pl.when(j == pl.num_programs(1) - 1)
    def _():
        o_ref[...] = res_ref[...] + gate_ref[...] * o_ref[...]


def _ffn_dense(h, w1, w3, w2, res, mod_l, *, seq, tm, tf, unit):
    n, d = h.shape
    n_f = w1.shape[1]
    tpb = seq // tm
    return pl.pallas_call(
        functools.partial(_ffn_dense_kernel, unit=unit),
        grid=(n // tm, n_f // tf),
        in_specs=[
            pl.BlockSpec((tm, d), lambda i, j: (i, 0)),
            pl.BlockSpec((d, tf), lambda i, j: (0, j)),
            pl.BlockSpec((d, tf), lambda i, j: (0, j)),
            pl.BlockSpec((tf, d), lambda i, j: (j, 0)),
            pl.BlockSpec((tm, d), lambda i, j: (i, 0), pipeline_mode=pl.Buffered(1)),
            pl.BlockSpec((None, None, 1, d), lambda i, j: (i // tpb, 5, 0, 0)),
        ],
        out_specs=pl.BlockSpec((tm, d), lambda i, j: (i, 0)),
        out_shape=jax.ShapeDtypeStruct((n, d), F32),
        scratch_shapes=[pltpu.VMEM((d, tf), BF16), pltpu.VMEM((d, tf), BF16), pltpu.VMEM((tf, d), BF16)],
        compiler_params=_cparams(("arbitrary", "arbitrary")),
        name="ffn_dense",
    )(h, w1, w3, w2, res, mod_l)


def _row_copy(src_hbm, src_row, dst_ref, dst_row, sem):
    return pltpu.make_async_copy(src_hbm.at[pl.ds(src_row, 1), :], dst_ref.at[pl.ds(dst_row, 1), :], sem)


def _ffn_moe_kernel(te_ref, tv_ref, src_ref, x_hbm, w1_ref, w3_ref, w2_ref, o_ref, xg_sc, xb_sc, w1_sc, w3_sc,
                    w2_sc, sem, *, unit, rows_per_step):
    del te_ref
    i = pl.program_id(0)
    j = pl.program_id(1)
    tm = xb_sc.shape[0]
    n_units = tv_ref[i]

    def start_row(tile, r):
        _row_copy(x_hbm, src_ref[tile * tm + r], xg_sc, r, sem).start()

    @pl.when(jnp.logical_and(i == 0, j == 0))
    def _():
        lax.fori_loop(0, tv_ref[0] * unit, lambda r, c: (start_row(0, r), c)[1], 0)

    @pl.when(j == 0)
    def _():
        def wait_unit(u, c):
            r0 = pl.multiple_of(u * unit, unit)
            pltpu.make_async_copy(x_hbm.at[pl.ds(0, unit), :], xg_sc.at[pl.ds(r0, unit), :], sem).wait()
            return c

        lax.fori_loop(0, n_units, wait_unit, 0)

        def cast_unit(u, c):
            r0 = pl.multiple_of(u * unit, unit)
            xb_sc[pl.ds(r0, unit), :] = xg_sc[pl.ds(r0, unit), :].astype(BF16)
            return c

        lax.fori_loop(0, n_units, cast_unit, 0)
        o_ref[...] = jnp.zeros_like(o_ref)

    @pl.when(i + 1 < pl.num_programs(0))
    def _():
        n_next = tv_ref[i + 1] * unit
        for g in range(rows_per_step // DMA_GROUP):
            first = j * rows_per_step + g * DMA_GROUP

            @pl.when(first < n_next)
            def _(first=first):
                for r in range(DMA_GROUP):
                    start_row(i + 1, first + r)

    @pl.when(n_units > 0)
    def _():
        _swiglu_step(xb_sc, (w1_ref, w3_ref, w2_ref), (w1_sc, w3_sc, w2_sc), o_ref, n_units, unit)


def _ffn_moe(h, w1, w3, w2, tile_e, tile_v, src, *, tm, tf, unit):
    n_pad = src.shape[0]
    d = h.shape[1]
    n_f = w1.shape[2]
    nj = n_f // tf
    jj = lambda i, j, tv: jnp.where(tv[i] > 0, j, nj - 1)
    rows_per_step = -(-tm // (nj * DMA_GROUP)) * DMA_GROUP
    assert unit % DMA_GROUP == 0 and tm % unit == 0
    return pl.pallas_call(
        functools.partial(_ffn_moe_kernel, unit=unit, rows_per_step=rows_per_step),
        grid_spec=pltpu.PrefetchScalarGridSpec(
            num_scalar_prefetch=3,
            grid=(n_pad // tm, nj),
            in_specs=[
                pl.BlockSpec(memory_space=pl.ANY),
                pl.BlockSpec((None, d, tf), lambda i, j, te, tv, src: (te[i], 0, jj(i, j, tv))),
                pl.BlockSpec((None, d, tf), lambda i, j, te, tv, src: (te[i], 0, jj(i, j, tv))),
                pl.BlockSpec((None, tf, d), lambda i, j, te, tv, src: (te[i], jj(i, j, tv), 0)),
            ],
            out_specs=pl.BlockSpec((tm, d), lambda i, j, te, tv, src: (i, 0)),
            scratch_shapes=[pltpu.VMEM((tm, d), h.dtype), pltpu.VMEM((tm, d), BF16),
                            pltpu.VMEM((d, tf), BF16), pltpu.VMEM((d, tf), BF16), pltpu.VMEM((tf, d), BF16),
                            pltpu.SemaphoreType.DMA(())],
        ),
        out_shape=jax.ShapeDtypeStruct((n_pad, d), F32),
        compiler_params=_cparams(("arbitrary", "arbitrary")),
        name="ffn_moe",
    )(tile_e, tile_v, src, h, w1, w3, w2)


def _combine_kernel(pos0_ref, pos1_ref, ys_hbm, x_ref, gate_ref, w0_ref, w1_ref, o_ref, buf0, buf1, sem):
    tc = x_ref.shape[0]
    base = pl.program_id(0) * tc

    def issue(g, c):
        for u in range(DMA_GROUP):
            r = g * DMA_GROUP + u
            _row_copy(ys_hbm, pos0_ref[base + r], buf0, r, sem).start()
            _row_copy(ys_hbm, pos1_ref[base + r], buf1, r, sem).start()
        return c

    lax.fori_loop(0, tc // DMA_GROUP, issue, 0)
    for buf in (buf0, buf1):
        pltpu.make_async_copy(ys_hbm.at[pl.ds(0, tc), :], buf, sem).wait()
    o_ref[...] = x_ref[...] + gate_ref[...] * (w0_ref[...] * buf0[...] + w1_ref[...] * buf1[...])


def _combine(pos0, pos1, ys, x2, mod_l, w0, w1, *, seq, tc):
    t, d = x2.shape
    tpb = seq // tc
    return pl.pallas_call(
        _combine_kernel,
        grid_spec=pltpu.PrefetchScalarGridSpec(
            num_scalar_prefetch=2,
            grid=(t // tc,),
            in_specs=[pl.BlockSpec(memory_space=pl.ANY),
                      pl.BlockSpec((tc, d), lambda i, p0, p1: (i, 0)),
                      pl.BlockSpec((None, None, 1, d), lambda i, p0, p1: (i // tpb, 5, 0, 0)),
                      pl.BlockSpec((tc, 1), lambda i, p0, p1: (i, 0)),
                      pl.BlockSpec((tc, 1), lambda i, p0, p1: (i, 0))],
            out_specs=pl.BlockSpec((tc, d), lambda i, p0, p1: (i, 0)),
            scratch_shapes=[pltpu.VMEM((tc, d), F32), pltpu.VMEM((tc, d), F32), pltpu.SemaphoreType.DMA(())],
        ),
        out_shape=jax.ShapeDtypeStruct((t, d), F32),
        compiler_params=_cparams(("arbitrary",)),
        name="moe_combine",
    )(pos0, pos1, ys, x2, mod_l, w0, w1)


def _route_metadata(route, n_experts, tm, unit):
    t = route.shape[0]
    e = route[:, :TOP_K].astype(jnp.int32)
    w = route[:, TOP_K:2 * TOP_K]
    cnt = jnp.sum((e[:, :, None] == jnp.arange(n_experts, dtype=jnp.int32)).astype(jnp.int32), axis=1)
    csum = jnp.cumsum(cnt, axis=0)
    rank = csum - cnt
    counts = csum[-1]
    padded = ((counts + tm - 1) // tm) * tm
    ends = jnp.cumsum(padded)
    starts = ends - padded
    pos = starts[e] + jnp.take_along_axis(rank, e, axis=1)
    n_tiles = (TOP_K * t + n_experts * (tm - 1)) // tm
    n_pad = n_tiles * tm
    tok = jnp.broadcast_to(jnp.arange(t, dtype=jnp.int32)[:, None], (t, TOP_K))
    flat = pos.reshape(-1)
    src = jnp.zeros((n_pad,), jnp.int32).at[flat].set(tok.reshape(-1), unique_indices=True)
    tile_start = jnp.arange(n_tiles, dtype=jnp.int32) * tm
    tile_e = jnp.sum((tile_start[:, None] >= ends[None, :]).astype(jnp.int32), axis=1)
    tile_e = jnp.minimum(tile_e, n_experts - 1)
    real_rows = jnp.clip(starts[tile_e] + counts[tile_e] - tile_start, 0, tm)
    tile_v = ((real_rows + unit - 1) // unit).astype(jnp.int32)
    return pos[:, 0], pos[:, 1], w[:, 0:1], w[:, 1:2], src, tile_e, tile_v


def kernel(x, c, w_ada, b_ada, norm1_g, norm2_g, w_in, q_norm_g, k_norm_g, sg_norm_g, w_s, b_s, w_proj_sb,
           w_proj_sg, w_out, ffn_w1, ffn_w3, ffn_w2, moe_router, moe_w1, moe_w3, moe_w2):
    batch, seq, d = x.shape
    depth = w_ada.shape[0]
    head_dim = q_norm_g.shape[1]
    width = w_proj_sb.shape[1]
    n_heads = width // head_dim
    n_groups, chunk = w_s.shape[1], w_s.shape[2]
    gdim = width // n_groups
    n_experts = moe_router.shape[2]
    assert sg_norm_g.shape[1] == width and d % width == 0 and w_in.shape[2] == 5 * width + 2 * d
    assert head_dim % LANES == 0 and gdim % LANES == 0 and n_experts >= 2 * TOP_K
    cfg = _tiles(seq, d)
    t = batch * seq

    mod = _adaln(c, w_ada, b_ada, cfg["ada_tn"]).reshape(depth, batch, N_MOD, 1, d)
    x2 = x.reshape(t, d)
    for i in range(depth):
        mod_l = mod[i]
        p = _in_proj(x2, mod_l, norm1_g[i], w_in, i, q_norm_g[i], k_norm_g[i], sg_norm_g[i],
                     seq=seq, tm=cfg["in_tm"], width=width, head_dim=head_dim)
        ysb = _sb_attention(p, batch=batch, seq=seq, n_heads=n_heads, head_dim=head_dim, tq=cfg["attn_tq"])
        bs_b = jnp.broadcast_to(b_s[i][:, :, None], (n_groups, chunk, gdim))
        moe = i % 2 == 1
        outs = _merge(ysb, p, x2, mod_l, norm2_g[i], w_s[i], bs_b, w_proj_sb[i].astype(BF16),
                      w_proj_sg[i].astype(BF16), w_out[i].astype(BF16), moe_router[i // 2] if moe else None,
                      seq=seq, tm=cfg["merge_tm"], width=width)
        if not moe:
            x2, h = outs
            jd = i // 2
            x2 = _ffn_dense(h, ffn_w1[jd], ffn_w3[jd], ffn_w2[jd], x2, mod_l, seq=seq,
                            tm=cfg["ffn_tm"], tf=cfg["ffn_tf"], unit=cfg["ffn_unit"])
        else:
            x2, h, route = outs
            jm = i // 2
            pos0, pos1, w0, w1, src, tile_e, tile_v = _route_metadata(route, n_experts, cfg["moe_tm"],
                                                                      cfg["ffn_unit"])
            ys = _ffn_moe(h, moe_w1[jm], moe_w3[jm], moe_w2[jm], tile_e, tile_v, src,
                          tm=cfg["moe_tm"], tf=cfg["ffn_tf"], unit=cfg["ffn_unit"])
            x2 = _combine(pos0, pos1, ys, x2, mod_l, w0, w1, seq=seq, tc=cfg["gather_rows"])
    return x2.reshape(batch, seq, d)
```

```python
import functools

import jax
import jax.numpy as jnp
from jax import lax
from jax.experimental import pallas as pl
from jax.experimental.pallas import tpu as pltpu

F32 = jnp.float32
BF16 = jnp.bfloat16
EPS = 1e-6
N_MOD = 6
TOP_K = 2
LOG2E = 1.4426950408889634
ATTN_UNROLL = 4
ATTN_HEADS = 2
LANES = 128
MXU_N = 256
DMA_GROUP = 8
VMEM_LIMIT = 56 * 1024 * 1024


def _tiles(seq, d_model):
    return dict(
        ada_tn=min(1024, d_model),
        in_tm=min(1024, seq),
        attn_tq=min(256, seq),
        merge_tm=min(256, seq),
        ffn_tm=min(1024, seq),
        ffn_unit=min(128, seq // 4),
        moe_tm=min(1024, seq) + min(128, seq // 4),
        ffn_tf=256,
        gather_rows=min(256, seq),
    )


def _cparams(sem):
    return pltpu.CompilerParams(dimension_semantics=sem, vmem_limit_bytes=VMEM_LIMIT)


def _adaln_kernel(cb_ref, w_ref, b_ref, o_ref):
    tn = w_ref.shape[1]
    for b in range(cb_ref.shape[0]):
        cb = cb_ref[b]
        act = cb * jax.nn.sigmoid(cb)
        for j in range(tn // LANES):
            sl = slice(j * LANES, (j + 1) * LANES)
            s = jnp.sum(w_ref[:, sl] * act, axis=0, keepdims=True)
            o_ref[b:b + 1, sl] = s + b_ref[:, sl]


def _adaln(c, w_ada, b_ada, tn):
    depth, k, n = w_ada.shape
    nb = c.shape[0]
    cb = jnp.broadcast_to(c[:, :, None], (nb, k, LANES))
    return pl.pallas_call(
        _adaln_kernel,
        grid=(depth, n // tn),
        in_specs=[
            pl.BlockSpec((nb, k, LANES), lambda l, j: (0, 0, 0)),
            pl.BlockSpec((None, k, tn), lambda l, j: (l, 0, j)),
            pl.BlockSpec((None, 1, tn), lambda l, j: (l, 0, j)),
        ],
        out_specs=pl.BlockSpec((None, nb, tn), lambda l, j: (l, 0, j)),
        out_shape=jax.ShapeDtypeStruct((depth, nb, n), F32),
        compiler_params=_cparams(("arbitrary", "arbitrary")),
        name="adaln",
    )(cb, w_ada, b_ada.reshape(depth, 1, n))


def _in_proj_kernel(x_ref, shift_ref, scale_ref, g_ref, w_ref, qg_ref, kg_ref, sgg_ref, o_ref, h_sc, gv_sc,
                    *, head_dim, qk_scale):
    j = pl.program_id(1)

    @pl.when(j == 0)
    def _():
        x = x_ref[...]
        y = x * lax.rsqrt(jnp.mean(x * x, axis=-1, keepdims=True) + EPS)
        h_sc[...] = ((y * g_ref[...]) * (1.0 + scale_ref[...]) + shift_ref[...]).astype(BF16)

    tn = o_ref.shape[1]

    def project(epilogue):
        for c in range(tn // MXU_N):
            sl = slice(c * MXU_N, (c + 1) * MXU_N)
            y = jnp.dot(h_sc[...], w_ref[:, sl].astype(BF16), preferred_element_type=F32)
            epilogue(y, c * MXU_N)

    def head_norm(gain):
        def epilogue(y, c0):
            for h in range(MXU_N // head_dim):
                yh = y[:, h * head_dim:(h + 1) * head_dim]
                r = lax.rsqrt(jnp.mean(yh * yh, axis=-1, keepdims=True) + EPS)
                o_ref[:, c0 + h * head_dim:c0 + (h + 1) * head_dim] = ((yh * r) * gain).astype(BF16)
        return epilogue

    def elementwise(fn):
        def epilogue(y, c0):
            o_ref[:, c0:c0 + MXU_N] = fn(y).astype(BF16)
        return epilogue

    @pl.when(j == 0)
    def _():
        project(head_norm(qg_ref[...] * qk_scale))

    @pl.when(j == 1)
    def _():
        project(head_norm(kg_ref[...]))

    @pl.when(j == 2)
    def _():
        project(elementwise(lambda y: y))

    @pl.when(j == 3)
    def _():
        project(elementwise(jax.nn.gelu))

    @pl.when(j == 4)
    def _():
        ssq = []

        def epilogue(y, c0):
            gv = jax.nn.gelu(y)
            gv_sc[:, c0:c0 + MXU_N] = gv
            ssq.append(jnp.sum(gv * gv, axis=-1, keepdims=True))

        project(epilogue)
        r = lax.rsqrt(sum(ssq) / tn + EPS)
        o_ref[...] = ((gv_sc[...] * r) * sgg_ref[...]).astype(BF16)

    @pl.when(j >= 5)
    def _():
        project(elementwise(jax.nn.sigmoid))


def _in_proj(x2, mod_l, norm_g, w_in, layer, q_g, k_g, sg_g, *, seq, tm, width, head_dim):
    t, d = x2.shape
    n = w_in.shape[2]
    tpb = seq // tm
    vec = lambda which: pl.BlockSpec((None, None, 1, d), lambda i, j: (i // tpb, which, 0, 0))
    const = lambda shape: pl.BlockSpec(shape, lambda i, j: (0,) * len(shape))
    kern = functools.partial(_in_proj_kernel, head_dim=head_dim, qk_scale=float(head_dim) ** -0.5 * LOG2E)
    return pl.pallas_call(
        kern,
        grid=(t // tm, n // width),
        in_specs=[
            pl.BlockSpec((tm, d), lambda i, j: (i, 0), pipeline_mode=pl.Buffered(1)),
            vec(0), vec(1),
            const((1, d)),
            pl.BlockSpec((None, d, width), lambda i, j: (layer, 0, j)),
            const((1, head_dim)), const((1, head_dim)), const((1, width)),
        ],
        out_specs=pl.BlockSpec((tm, width), lambda i, j: (i, j)),
        out_shape=jax.ShapeDtypeStruct((t, n), BF16),
        scratch_shapes=[pltpu.VMEM((tm, d), BF16), pltpu.VMEM((tm, width), F32)],
        compiler_params=_cparams(("arbitrary", "arbitrary")),
        name="in_proj",
    )(x2, mod_l, mod_l, norm_g.reshape(1, d), w_in, q_g.reshape(1, head_dim), k_g.reshape(1, head_dim),
      sg_g.reshape(1, width))


def _sb_attn_kernel(q_ref, k_ref, v_ref, o_ref, carry_sc, acc_sc, *, unroll, dh):
    i = pl.program_id(2)
    tq = q_ref.shape[0]
    heads = range(q_ref.shape[1] // dh)
    hs = lambda h: slice(h * dh, (h + 1) * dh)
    qs = [q_ref[:, hs(h)] for h in heads]
    row = lax.broadcasted_iota(jnp.int32, (tq, tq), 0)
    col = lax.broadcasted_iota(jnp.int32, (tq, tq), 1)
    suffix = (row >= col).astype(BF16)
    strict = col < row

    def scores(kb, h, diagonal):
        start = pl.multiple_of(kb * tq, tq)
        z = lax.dot_general(qs[h], k_ref[pl.ds(start, tq), hs(h)], (((1,), (1,)), ((), ())),
                            preferred_element_type=F32)
        sp = jnp.maximum(z, 0.0) + jnp.log2(1.0 + jnp.exp2(-jnp.abs(z)))
        if diagonal:
            sp = jnp.where(strict, sp, 0.0)
        incl = jnp.dot(sp.astype(BF16), suffix, preferred_element_type=F32)
        return start, h, z, incl, diagonal

    def accumulate(parts, state):
        carry = [state[2 * h] for h in heads]
        acc = [state[2 * h + 1] for h in heads]
        for start, h, z, incl, diagonal in parts:
            w = jnp.exp2(z - incl - carry[h])
            if diagonal:
                w = jnp.where(strict, w, 0.0)
            acc[h] = acc[h] + jnp.dot(w.astype(BF16), v_ref[pl.ds(start, tq), hs(h)],
                                      preferred_element_type=F32)
            carry[h] = carry[h] + incl[:, 0:1]
        return tuple(x for h in heads for x in (carry[h], acc[h]))

    rem = i % unroll
    for r in range(unroll):
        @pl.when(rem == r)
        def _(r=r):
            parts = [scores(i, h, True) for h in heads]
            parts += [scores(i - 1 - u, h, False) for u in range(r) for h in heads]
            zero = tuple(x for h in heads for x in (jnp.zeros((tq, 1), F32), jnp.zeros((tq, dh), F32)))
            state = accumulate(parts, zero)
            for h in heads:
                carry_sc[h] = state[2 * h]
                acc_sc[h] = state[2 * h + 1]

    def group(s, state):
        first = i - rem - 1 - s * unroll
        return accumulate([scores(first - u, h, False) for u in range(unroll) for h in heads], state)

    state = tuple(x for h in heads for x in (carry_sc[h], acc_sc[h]))
    state = lax.fori_loop(0, i // unroll, group, state)
    for h in heads:
        o_ref[:, hs(h)] = state[2 * h + 1].astype(BF16)


def _sb_attention(p, *, batch, seq, n_heads, head_dim, tq):
    t = p.shape[0]
    nq = seq // tq
    nh = ATTN_HEADS
    assert n_heads % nh == 0
    ng = n_heads // nh
    wide = nh * head_dim
    return pl.pallas_call(
        functools.partial(_sb_attn_kernel, unroll=ATTN_UNROLL, dh=head_dim),
        grid=(batch, ng, nq),
        in_specs=[
            pl.BlockSpec((tq, wide), lambda b, g, i: (b * nq + i, g)),
            pl.BlockSpec((seq, wide), lambda b, g, i: (b, ng + g)),
            pl.BlockSpec((seq, wide), lambda b, g, i: (b, 2 * ng + g)),
        ],
        out_specs=pl.BlockSpec((tq, wide), lambda b, g, i: (b * nq + i, g)),
        out_shape=jax.ShapeDtypeStruct((t, n_heads * head_dim), BF16),
        scratch_shapes=[pltpu.VMEM((nh, tq, 1), F32), pltpu.VMEM((nh, tq, head_dim), F32)],
        compiler_params=_cparams(("arbitrary", "arbitrary", "arbitrary")),
        name="sb_attention",
    )(p, p, p)


def _route_top2(logits):
    n_e = logits.shape[1]
    lane = lax.broadcasted_iota(jnp.int32, logits.shape, 1)
    m1 = jnp.max(logits, axis=-1, keepdims=True)
    i1 = jnp.min(jnp.where(logits == m1, lane, n_e), axis=-1, keepdims=True)
    rest = jnp.where(lane == i1, -jnp.inf, logits)
    m2 = jnp.max(rest, axis=-1, keepdims=True)
    i2 = jnp.min(jnp.where(rest == m2, lane, n_e), axis=-1, keepdims=True)
    t = jnp.exp(m2 - m1)
    w1 = 1.0 / (1.0 + t)
    w2 = t / (1.0 + t)
    out = jnp.where(lane == 0, i1.astype(F32), 0.0)
    out = jnp.where(lane == 1, i2.astype(F32), out)
    out = jnp.where(lane == 2, w1, out)
    out = jnp.where(lane == 3, w2, out)
    return out


def _merge_kernel(*refs, chunk, gdim, moe, dw):
    ysb_ref, su_ref, sv_ref = refs[:3]
    ga_refs = refs[3:3 + dw]
    gb_refs = refs[3 + dw:3 + 2 * dw]
    refs = refs[3 + 2 * dw:]
    x_ref, gate1_ref, shift2_ref, scale2_ref, g2_ref, ws_ref, bs_ref, pa_ref, pb_ref, wo_ref = refs[:10]
    if moe:
        wr_ref, xo_ref, h_ref, route_ref, ysg_sc, m_sc = refs[10:]
    else:
        xo_ref, h_ref, ysg_sc, m_sc = refs[10:]
    tm = x_ref.shape[0]
    width = ysb_ref.shape[1]
    n_groups = ws_ref.shape[0]
    row = lax.broadcasted_iota(jnp.int32, (chunk, chunk), 0)
    col = lax.broadcasted_iota(jnp.int32, (chunk, chunk), 1)
    causal = col <= row
    for g in range(n_groups):
        cs = slice(g * gdim, (g + 1) * gdim)
        wg = jnp.where(causal, ws_ref[g], 0.0).astype(BF16)
        for ch in range(tm // chunk):
            rs = slice(ch * chunk, (ch + 1) * chunk)
            mixed = jnp.dot(wg, sv_ref[rs, cs], preferred_element_type=F32) + bs_ref[g]
            ysg_sc[rs, cs] = (su_ref[rs, cs].astype(F32) * mixed).astype(BF16)
    for s in range(dw):
        sl = slice(s * width, (s + 1) * width)
        a = jnp.dot(ysb_ref[...], pa_ref[:, sl], preferred_element_type=F32)
        b = jnp.dot(ysg_sc[...], pb_ref[:, sl], preferred_element_type=F32)
        m_sc[:, sl] = (ga_refs[s][...].astype(F32) * a + gb_refs[s][...].astype(F32) * b).astype(BF16)
    o = jnp.dot(m_sc[...], wo_ref[...], preferred_element_type=F32)
    xn = x_ref[...] + gate1_ref[...] * o
    xo_ref[...] = xn
    y = xn * lax.rsqrt(jnp.mean(xn * xn, axis=-1, keepdims=True) + EPS)
    h = (y * g2_ref[...]) * (1.0 + scale2_ref[...]) + shift2_ref[...]
    h_ref[...] = h.astype(h_ref.dtype)
    if moe:
        wr = wr_ref[...]
        h_hi, w_hi = h.astype(BF16), wr.astype(BF16)
        h_lo = (h - h_hi.astype(F32)).astype(BF16)
        w_lo = (wr - w_hi.astype(F32)).astype(BF16)
        logits = (jnp.dot(h_hi, w_hi, preferred_element_type=F32)
                  + jnp.dot(h_lo, w_hi, preferred_element_type=F32)
                  + jnp.dot(h_hi, w_lo, preferred_element_type=F32))
        route_ref[...] = _route_top2(logits)


def _merge(ysb, p, x2, mod_l, norm2_g, w_s, bs_b, pa_bf, pb_bf, wo_bf, w_router, *, seq, tm, width):
    t, d = x2.shape
    n_groups, chunk, _ = w_s.shape
    gdim = width // n_groups
    tpb = seq // tm
    moe = w_router is not None
    dw = d // width
    vec = lambda which: pl.BlockSpec((None, None, 1, d), lambda i: (i // tpb, which, 0, 0))
    resident = lambda shape: pl.BlockSpec(shape, lambda i: (0,) * len(shape), pipeline_mode=pl.Buffered(1))
    pcol = lambda cb: pl.BlockSpec((tm, width), lambda i: (i, cb))
    in_specs = [
        pl.BlockSpec((tm, width), lambda i: (i, 0)),
        pcol(3),
        pcol(4),
        *[pcol(5 + s) for s in range(dw)],
        *[pcol(5 + dw + s) for s in range(dw)],
        pl.BlockSpec((tm, d), lambda i: (i, 0)),
        vec(2), vec(3), vec(4),
        resident((1, d)),
        resident((n_groups, chunk, chunk)),
        resident((n_groups, chunk, gdim)),
        resident((width, d)), resident((width, d)), resident((d, d)),
    ]
    args = [ysb, p, p, *([p] * (2 * dw)), x2, mod_l, mod_l, mod_l, norm2_g.reshape(1, d), w_s, bs_b,
            pa_bf, pb_bf, wo_bf]
    out_specs = [pl.BlockSpec((tm, d), lambda i: (i, 0)), pl.BlockSpec((tm, d), lambda i: (i, 0))]
    out_shape = [jax.ShapeDtypeStruct((t, d), F32), jax.ShapeDtypeStruct((t, d), F32 if moe else BF16)]
    if moe:
        n_e = w_router.shape[1]
        in_specs.append(resident((d, n_e)))
        args.append(w_router)
        out_specs.append(pl.BlockSpec((tm, n_e), lambda i: (i, 0)))
        out_shape.append(jax.ShapeDtypeStruct((t, n_e), F32))
    kern = functools.partial(_merge_kernel, chunk=chunk, gdim=gdim, moe=moe, dw=dw)
    return pl.pallas_call(
        kern,
        grid=(t // tm,),
        in_specs=in_specs,
        out_specs=out_specs,
        out_shape=out_shape,
        scratch_shapes=[pltpu.VMEM((tm, width), BF16), pltpu.VMEM((tm, d), BF16)],
        compiler_params=_cparams(("arbitrary",)),
        name="merge_moe" if moe else "merge_dense",
    )(*args)


def _swiglu_step(x_ref, w_refs, w_scs, o_ref, n_units, unit):
    w1_sc, w3_sc, w2_sc = w_scs
    big = 4 * unit

    def cast_weights():
        for w_ref, w_sc in zip(w_refs, w_scs):
            w_sc[...] = w_ref[...].astype(BF16)

    def rows(r0, m):
        xs = x_ref[pl.ds(r0, m), :]
        a = jnp.dot(xs, w1_sc[...], preferred_element_type=F32)
        b = jnp.dot(xs, w3_sc[...], preferred_element_type=F32)
        hh = (jax.nn.silu(a) * b).astype(BF16)
        o_ref[pl.ds(r0, m), :] += jnp.dot(hh, w2_sc[...], preferred_element_type=F32)

    if isinstance(n_units, int):
        assert n_units % 4 == 0
        cast_weights()
        for s in range(n_units // 4):
            rows(s * big, big)
        return

    n_big = n_units // 4
    rem = n_units % 4
    for nb in range(x_ref.shape[0] // big + 1):
        @pl.when(n_big == nb)
        def _(nb=nb):
            cast_weights()
            for s in range(nb):
                rows(s * big, big)

    @pl.when(rem >= 2)
    def _():
        rows(pl.multiple_of(n_big * big, unit), 2 * unit)

    @pl.when(rem % 2 == 1)
    def _():
        rows(pl.multiple_of(n_big * big + (rem // 2) * 2 * unit, unit), unit)


def _ffn_dense_kernel(x_ref, w1_ref, w3_ref, w2_ref, res_ref, gate_ref, o_ref, w1_sc, w3_sc, w2_sc, *, unit):
    j = pl.program_id(1)

    @pl.when(j == 0)
    def _():
        o_ref[...] = jnp.zeros_like(o_ref)

    _swiglu_step(x_ref, (w1_ref, w3_ref, w2_ref), (w1_sc, w3_sc, w2_sc), o_ref, x_ref.shape[0] // unit, unit)

    @pl.when(j == pl.num_programs(1) - 1)
    def _():
        o_ref[...] = res_ref[...] + gate_ref[...] * o_ref[...]


def _ffn_dense(h, w1, w3, w2, res, mod_l, *, seq, tm, tf, unit):
    n, d = h.shape
    n_f = w1.shape[1]
    tpb = seq // tm
    return pl.pallas_call(
        functools.partial(_ffn_dense_kernel, unit=unit),
        grid=(n // tm, n_f // tf),
        in_specs=[
            pl.BlockSpec((tm, d), lambda i, j: (i, 0)),
            pl.BlockSpec((d, tf), lambda i, j: (0, j)),
            pl.BlockSpec((d, tf), lambda i, j: (0, j)),
            pl.BlockSpec((tf, d), lambda i, j: (j, 0)),
            pl.BlockSpec((tm, d), lambda i, j: (i, 0), pipeline_mode=pl.Buffered(1)),
            pl.BlockSpec((None, None, 1, d), lambda i, j: (i // tpb, 5, 0, 0)),
        ],
        out_specs=pl.BlockSpec((tm, d), lambda i, j: (i, 0)),
        out_shape=jax.ShapeDtypeStruct((n, d), F32),
        scratch_shapes=[pltpu.VMEM((d, tf), BF16), pltpu.VMEM((d, tf), BF16), pltpu.VMEM((tf, d), BF16)],
        compiler_params=_cparams(("arbitrary", "arbitrary")),
        name="ffn_dense",
    )(h, w1, w3, w2, res, mod_l)


def _row_copy(src_hbm, src_row, dst_ref, dst_row, sem):
    return pltpu.make_async_copy(src_hbm.at[pl.ds(src_row, 1), :], dst_ref.at[pl.ds(dst_row, 1), :], sem)


def _ffn_moe_kernel(te_ref, tv_ref, src_ref, x_hbm, w1_ref, w3_ref, w2_ref, o_ref, xg_sc, xb_sc, w1_sc, w3_sc,
                    w2_sc, sem, *, unit, rows_per_step):
    del te_ref
    i = pl.program_id(0)
    j = pl.program_id(1)
    tm = xb_sc.shape[0]
    n_units = tv_ref[i]

    def start_row(tile, r):
        _row_copy(x_hbm, src_ref[tile * tm + r], xg_sc, r, sem).start()

    @pl.when(jnp.logical_and(i == 0, j == 0))
    def _():
        lax.fori_loop(0, tv_ref[0] * unit, lambda r, c: (start_row(0, r), c)[1], 0)

    @pl.when(j == 0)
    def _():
        def wait_unit(u, c):
            r0 = pl.multiple_of(u * unit, unit)
            pltpu.make_async_copy(x_hbm.at[pl.ds(0, unit), :], xg_sc.at[pl.ds(r0, unit), :], sem).wait()
            return c

        lax.fori_loop(0, n_units, wait_unit, 0)

        def cast_unit(u, c):
            r0 = pl.multiple_of(u * unit, unit)
            xb_sc[pl.ds(r0, unit), :] = xg_sc[pl.ds(r0, unit), :].astype(BF16)
            return c

        lax.fori_loop(0, n_units, cast_unit, 0)
        o_ref[...] = jnp.zeros_like(o_ref)

    @pl.when(i + 1 < pl.num_programs(0))
    def _():
        n_next = tv_ref[i + 1] * unit
        for g in range(rows_per_step // DMA_GROUP):
            first = j * rows_per_step + g * DMA_GROUP

            @pl.when(first < n_next)
            def _(first=first):
                for r in range(DMA_GROUP):
                    start_row(i + 1, first + r)

    @pl.when(n_units > 0)
    def _():
        _swiglu_step(xb_sc, (w1_ref, w3_ref, w2_ref), (w1_sc, w3_sc, w2_sc), o_ref, n_units, unit)


def _ffn_moe(h, w1, w3, w2, tile_e, tile_v, src, *, tm, tf, unit):
    n_pad = src.shape[0]
    d = h.shape[1]
    n_f = w1.shape[2]
    nj = n_f // tf
    jj = lambda i, j, tv: jnp.where(tv[i] > 0, j, nj - 1)
    rows_per_step = -(-tm // (nj * DMA_GROUP)) * DMA_GROUP
    assert unit % DMA_GROUP == 0 and tm % unit == 0
    return pl.pallas_call(
        functools.partial(_ffn_moe_kernel, unit=unit, rows_per_step=rows_per_step),
        grid_spec=pltpu.PrefetchScalarGridSpec(
            num_scalar_prefetch=3,
            grid=(n_pad // tm, nj),
            in_specs=[
                pl.BlockSpec(memory_space=pl.ANY),
                pl.BlockSpec((None, d, tf), lambda i, j, te, tv, src: (te[i], 0, jj(i, j, tv))),
                pl.BlockSpec((None, d, tf), lambda i, j, te, tv, src: (te[i], 0, jj(i, j, tv))),
                pl.BlockSpec((None, tf, d), lambda i, j, te, tv, src: (te[i], jj(i, j, tv), 0)),
            ],
            out_specs=pl.BlockSpec((tm, d), lambda i, j, te, tv, src: (i, 0)),
            scratch_shapes=[pltpu.VMEM((tm, d), h.dtype), pltpu.VMEM((tm, d), BF16),
                            pltpu.VMEM((d, tf), BF16), pltpu.VMEM((d, tf), BF16), pltpu.VMEM((tf, d), BF16),
                            pltpu.SemaphoreType.DMA(())],
        ),
        out_shape=jax.ShapeDtypeStruct((n_pad, d), F32),
        compiler_params=_cparams(("arbitrary", "arbitrary")),
        name="ffn_moe",
    )(tile_e, tile_v, src, h, w1, w3, w2)


def _combine_kernel(pos0_ref, pos1_ref, ys_hbm, x_ref, gate_ref, w0_ref, w1_ref, o_ref, buf0, buf1, sem):
    tc = x_ref.shape[0]
    base = pl.program_id(0) * tc

    def issue(g, c):
        for u in range(DMA_GROUP):
            r = g * DMA_GROUP + u
            _row_copy(ys_hbm, pos0_ref[base + r], buf0, r, sem).start()
            _row_copy(ys_hbm, pos1_ref[base + r], buf1, r, sem).start()
        return c

    lax.fori_loop(0, tc // DMA_GROUP, issue, 0)
    for buf in (buf0, buf1):
        pltpu.make_async_copy(ys_hbm.at[pl.ds(0, tc), :], buf, sem).wait()
    o_ref[...] = x_ref[...] + gate_ref[...] * (w0_ref[...] * buf0[...] + w1_ref[...] * buf1[...])


def _combine(pos0, pos1, ys, x2, mod_l, w0, w1, *, seq, tc):
    t, d = x2.shape
    tpb = seq // tc
    return pl.pallas_call(
        _combine_kernel,
        grid_spec=pltpu.PrefetchScalarGridSpec(
            num_scalar_prefetch=2,
            grid=(t // tc,),
            in_specs=[pl.BlockSpec(memory_space=pl.ANY),
                      pl.BlockSpec((tc, d), lambda i, p0, p1: (i, 0)),
                      pl.BlockSpec((None, None, 1, d), lambda i, p0, p1: (i // tpb, 5, 0, 0)),
                      pl.BlockSpec((tc, 1), lambda i, p0, p1: (i, 0)),
                      pl.BlockSpec((tc, 1), lambda i, p0, p1: (i, 0))],
            out_specs=pl.BlockSpec((tc, d), lambda i, p0, p1: (i, 0)),
            scratch_shapes=[pltpu.VMEM((tc, d), F32), pltpu.VMEM((tc, d), F32), pltpu.SemaphoreType.DMA(())],
        ),
        out_shape=jax.ShapeDtypeStruct((t, d), F32),
        compiler_params=_cparams(("arbitrary",)),
        name="moe_combine",
    )(pos0, pos1, ys, x2, mod_l, w0, w1)


def _route_metadata(route, n_experts, tm, unit):
    t = route.shape[0]
    e = route[:, :TOP_K].astype(jnp.int32)
    w = route[:, TOP_K:2 * TOP_K]
    cnt = jnp.sum((e[:, :, None] == jnp.arange(n_experts, dtype=jnp.int32)).astype(jnp.int32), axis=1)
    csum = jnp.cumsum(cnt, axis=0)
    rank = csum - cnt
    counts = csum[-1]
    padded = ((counts + tm - 1) // tm) * tm
    ends = jnp.cumsum(padded)
    starts = ends - padded
    pos = starts[e] + jnp.take_along_axis(rank, e, axis=1)
    n_tiles = (TOP_K * t + n_experts * (tm - 1)) // tm
    n_pad = n_tiles * tm
    tok = jnp.broadcast_to(jnp.arange(t, dtype=jnp.int32)[:, None], (t, TOP_K))
    flat = pos.reshape(-1)
    src = jnp.zeros((n_pad,), jnp.int32).at[flat].set(tok.reshape(-1), unique_indices=True)
    tile_start = jnp.arange(n_tiles, dtype=jnp.int32) * tm
    tile_e = jnp.sum((tile_start[:, None] >= ends[None, :]).astype(jnp.int32), axis=1)
    tile_e = jnp.minimum(tile_e, n_experts - 1)
    real_rows = jnp.clip(starts[tile_e] + counts[tile_e] - tile_start, 0, tm)
    tile_v = ((real_rows + unit - 1) // unit).astype(jnp.int32)
    return pos[:, 0], pos[:, 1], w[:, 0:1], w[:, 1:2], src, tile_e, tile_v


def kernel(x, c, w_ada, b_ada, norm1_g, norm2_g, w_in, q_norm_g, k_norm_g, sg_norm_g, w_s, b_s, w_proj_sb,
           w_proj_sg, w_out, ffn_w1, ffn_w3, ffn_w2, moe_router, moe_w1, moe_w3, moe_w2):
    batch, seq, d = x.shape
    depth = w_ada.shape[0]
    head_dim = q_norm_g.shape[1]
    width = w_proj_sb.shape[1]
    n_heads = width // head_dim
    n_groups, chunk = w_s.shape[1], w_s.shape[2]
    gdim = width // n_groups
    n_experts = moe_router.shape[2]
    assert sg_norm_g.shape[1] == width and d % width == 0 and w_in.shape[2] == 5 * width + 2 * d
    assert head_dim % LANES == 0 and gdim % LANES == 0 and n_experts >= 2 * TOP_K
    cfg = _tiles(seq, d)
    t = batch * seq

    mod = _adaln(c, w_ada, b_ada, cfg["ada_tn"]).reshape(depth, batch, N_MOD, 1, d)
    x2 = x.reshape(t, d)
    for i in range(depth):
        mod_l = mod[i]
        p = _in_proj(x2, mod_l, norm1_g[i], w_in, i, q_norm_g[i], k_norm_g[i], sg_norm_g[i],
                     seq=seq, tm=cfg["in_tm"], width=width, head_dim=head_dim)
        ysb = _sb_attention(p, batch=batch, seq=seq, n_heads=n_heads, head_dim=head_dim, tq=cfg["attn_tq"])
        bs_b = jnp.broadcast_to(b_s[i][:, :, None], (n_groups, chunk, gdim))
        moe = i % 2 == 1
        outs = _merge(ysb, p, x2, mod_l, norm2_g[i], w_s[i], bs_b, w_proj_sb[i].astype(BF16),
                      w_proj_sg[i].astype(BF16), w_out[i].astype(BF16), moe_router[i // 2] if moe else None,
                      seq=seq, tm=cfg["merge_tm"], width=width)
        if not moe:
            x2, h = outs
            jd = i // 2
            x2 = _ffn_dense(h, ffn_w1[jd], ffn_w3[jd], ffn_w2[jd], x2, mod_l, seq=seq,
                            tm=cfg["ffn_tm"], tf=cfg["ffn_tf"], unit=cfg["ffn_unit"])
        else:
            x2, h, route = outs
            jm = i // 2
            pos0, pos1, w0, w1, src, tile_e, tile_v = _route_metadata(route, n_experts, cfg["moe_tm"],
                                                                      cfg["ffn_unit"])
            ys = _ffn_moe(h, moe_w1[jm], moe_w3[jm], moe_w2[jm], tile_e, tile_v, src,
                          tm=cfg["moe_tm"], tf=cfg["ffn_tf"], unit=cfg["ffn_unit"])
            x2 = _combine(pos0, pos1, ys, x2, mod_l, w0, w1, seq=seq, tc=cfg["gather_rows"])
    return x2.reshape(batch, seq, d)
```

```python
import functools

import jax
import jax.numpy as jnp
from jax import lax
from jax.experimental import pallas as pl
from jax.experimental.pallas import tpu as pltpu

F32 = jnp.float32
BF16 = jnp.bfloat16
EPS = 1e-6
N_MOD = 6
TOP_K = 2
LOG2E = 1.4426950408889634
ATTN_UNROLL = 4
ATTN_HEADS = 2
LANES = 128
MXU_N = 256
DMA_GROUP = 8
VMEM_LIMIT = 56 * 1024 * 1024


def _tiles(seq, d_model):
    return dict(
        ada_tn=min(1024, d_model),
        in_tm=min(1024, seq),
        attn_tq=min(256, seq),
        merge_tm=min(256, seq),
        ffn_tm=min(1024, seq),
        ffn_unit=min(128, seq // 4),
        moe_tm=min(1024, seq) + min(128, seq // 4),
        ffn_tf=256,
        gather_rows=min(256, seq),
    )


def _cparams(sem):
    return pltpu.CompilerParams(dimension_semantics=sem, vmem_limit_bytes=VMEM_LIMIT)


def _adaln_kernel(cb_ref, w_ref, b_ref, o_ref):
    tn = w_ref.shape[1]
    for b in range(cb_ref.shape[0]):
        cb = cb_ref[b]
        act = cb * jax.nn.sigmoid(cb)
        for j in range(tn // LANES):
            sl = slice(j * LANES, (j + 1) * LANES)
            s = jnp.sum(w_ref[:, sl] * act, axis=0, keepdims=True)
            o_ref[b:b + 1, sl] = s + b_ref[:, sl]


def _adaln(c, w_ada, b_ada, tn):
    depth, k, n = w_ada.shape
    nb = c.shape[0]
    cb = jnp.broadcast_to(c[:, :, None], (nb, k, LANES))
    return pl.pallas_call(
        _adaln_kernel,
        grid=(depth, n // tn),
        in_specs=[
            pl.BlockSpec((nb, k, LANES), lambda l, j: (0, 0, 0)),
            pl.BlockSpec((None, k, tn), lambda l, j: (l, 0, j)),
            pl.BlockSpec((None, 1, tn), lambda l, j: (l, 0, j)),
        ],
        out_specs=pl.BlockSpec((None, nb, tn), lambda l, j: (l, 0, j)),
        out_shape=jax.ShapeDtypeStruct((depth, nb, n), F32),
        compiler_params=_cparams(("arbitrary", "arbitrary")),
        name="adaln",
    )(cb, w_ada, b_ada.reshape(depth, 1, n))


def _in_proj_kernel(x_ref, shift_ref, scale_ref, g_ref, w_ref, qg_ref, kg_ref, sgg_ref, o_ref, h_sc, gv_sc,
                    *, head_dim, qk_scale):
    j = pl.program_id(1)

    @pl.when(j == 0)
    def _():
        x = x_ref[...]
        y = x * lax.rsqrt(jnp.mean(x * x, axis=-1, keepdims=True) + EPS)
        h_sc[...] = ((y * g_ref[...]) * (1.0 + scale_ref[...]) + shift_ref[...]).astype(BF16)

    tn = o_ref.shape[1]

    def project(epilogue):
        for c in range(tn // MXU_N):
            sl = slice(c * MXU_N, (c + 1) * MXU_N)
            y = jnp.dot(h_sc[...], w_ref[:, sl].astype(BF16), preferred_element_type=F32)
            epilogue(y, c * MXU_N)

    def head_norm(gain):
        def epilogue(y, c0):
            for h in range(MXU_N // head_dim):
                yh = y[:, h * head_dim:(h + 1) * head_dim]
                r = lax.rsqrt(jnp.mean(yh * yh, axis=-1, keepdims=True) + EPS)
                o_ref[:, c0 + h * head_dim:c0 + (h + 1) * head_dim] = ((yh * r) * gain).astype(BF16)
        return epilogue

    def elementwise(fn):
        def epilogue(y, c0):
            o_ref[:, c0:c0 + MXU_N] = fn(y).astype(BF16)
        return epilogue

    @pl.when(j == 0)
    def _():
        project(head_norm(qg_ref[...] * qk_scale))

    @pl.when(j == 1)
    def _():
        project(head_norm(kg_ref[...]))

    @pl.when(j == 2)
    def _():
        project(elementwise(lambda y: y))

    @pl.when(j == 3)
    def _():
        project(elementwise(jax.nn.gelu))

    @pl.when(j == 4)
    def _():
        ssq = []

        def epilogue(y, c0):
            gv = jax.nn.gelu(y)
            gv_sc[:, c0:c0 + MXU_N] = gv
            ssq.append(jnp.sum(gv * gv, axis=-1, keepdims=True))

        project(epilogue)
        r = lax.rsqrt(sum(ssq) / tn + EPS)
        o_ref[...] = ((gv_sc[...] * r) * sgg_ref[...]).astype(BF16)

    @pl.when(j >= 5)
    def _():
        project(elementwise(jax.nn.sigmoid))


def _in_proj(x2, mod_l, norm_g, w_in, layer, q_g, k_g, sg_g, *, seq, tm, width, head_dim):
    t, d = x2.shape
    n = w_in.shape[2]
    tpb = seq // tm
    vec = lambda which: pl.BlockSpec((None, None, 1, d), lambda i, j: (i // tpb, which, 0, 0))
    const = lambda shape: pl.BlockSpec(shape, lambda i, j: (0,) * len(shape))
    kern = functools.partial(_in_proj_kernel, head_dim=head_dim, qk_scale=float(head_dim) ** -0.5 * LOG2E)
    return pl.pallas_call(
        kern,
        grid=(t // tm, n // width),
        in_specs=[
            pl.BlockSpec((tm, d), lambda i, j: (i, 0), pipeline_mode=pl.Buffered(1)),
            vec(0), vec(1),
            const((1, d)),
            pl.BlockSpec((None, d, width), lambda i, j: (layer, 0, j)),
            const((1, head_dim)), const((1, head_dim)), const((1, width)),
        ],
        out_specs=pl.BlockSpec((tm, width), lambda i, j: (i, j)),
        out_shape=jax.ShapeDtypeStruct((t, n), BF16),
        scratch_shapes=[pltpu.VMEM((tm, d), BF16), pltpu.VMEM((tm, width), F32)],
        compiler_params=_cparams(("arbitrary", "arbitrary")),
        name="in_proj",
    )(x2, mod_l, mod_l, norm_g.reshape(1, d), w_in, q_g.reshape(1, head_dim), k_g.reshape(1, head_dim),
      sg_g.reshape(1, width))


def _sb_attn_kernel(q_ref, k_ref, v_ref, o_ref, carry_sc, acc_sc, *, unroll, dh):
    i = pl.program_id(2)
    tq = q_ref.shape[0]
    heads = range(q_ref.shape[1] // dh)
    hs = lambda h: slice(h * dh, (h + 1) * dh)
    qs = [q_ref[:, hs(h)] for h in heads]
    row = lax.broadcasted_iota(jnp.int32, (tq, tq), 0)
    col = lax.broadcasted_iota(jnp.int32, (tq, tq), 1)
    suffix = (row >= col).astype(BF16)
    strict = col < row

    def scores(kb, h, diagonal):
        start = pl.multiple_of(kb * tq, tq)
        z = lax.dot_general(qs[h], k_ref[pl.ds(start, tq), hs(h)], (((1,), (1,)), ((), ())),
                            preferred_element_type=F32)
        sp = jnp.maximum(z, 0.0) + jnp.log2(1.0 + jnp.exp2(-jnp.abs(z)))
        if diagonal:
            sp = jnp.where(strict, sp, 0.0)
        incl = jnp.dot(sp.astype(BF16), suffix, preferred_element_type=F32)
        return start, h, z, incl, diagonal

    def accumulate(parts, state):
        carry = [state[2 * h] for h in heads]
        acc = [state[2 * h + 1] for h in heads]
        for start, h, z, incl, diagonal in parts:
            w = jnp.exp2(z - incl - carry[h])
            if diagonal:
                w = jnp.where(strict, w, 0.0)
            acc[h] = acc[h] + jnp.dot(w.astype(BF16), v_ref[pl.ds(start, tq), hs(h)],
                                      preferred_element_type=F32)
            carry[h] = carry[h] + incl[:, 0:1]
        return tuple(x for h in heads for x in (carry[h], acc[h]))

    rem = i % unroll
    for r in range(unroll):
        @pl.when(rem == r)
        def _(r=r):
            parts = [scores(i, h, True) for h in heads]
            parts += [scores(i - 1 - u, h, False) for u in range(r) for h in heads]
            zero = tuple(x for h in heads for x in (jnp.zeros((tq, 1), F32), jnp.zeros((tq, dh), F32)))
            state = accumulate(parts, zero)
            for h in heads:
                carry_sc[h] = state[2 * h]
                acc_sc[h] = state[2 * h + 1]

    def group(s, state):
        first = i - rem - 1 - s * unroll
        return accumulate([scores(first - u, h, False) for u in range(unroll) for h in heads], state)

    state = tuple(x for h in heads for x in (carry_sc[h], acc_sc[h]))
    state = lax.fori_loop(0, i // unroll, group, state)
    for h in heads:
        o_ref[:, hs(h)] = state[2 * h + 1].astype(BF16)


def _sb_attention(p, *, batch, seq, n_heads, head_dim, tq):
    t = p.shape[0]
    nq = seq // tq
    nh = ATTN_HEADS
    assert n_heads % nh == 0
    ng = n_heads // nh
    wide = nh * head_dim
    return pl.pallas_call(
        functools.partial(_sb_attn_kernel, unroll=ATTN_UNROLL, dh=head_dim),
        grid=(batch, ng, nq),
        in_specs=[
            pl.BlockSpec((tq, wide), lambda b, g, i: (b * nq + i, g)),
            pl.BlockSpec((seq, wide), lambda b, g, i: (b, ng + g)),
            pl.BlockSpec((seq, wide), lambda b, g, i: (b, 2 * ng + g)),
        ],
        out_specs=pl.BlockSpec((tq, wide), lambda b, g, i: (b * nq + i, g)),
        out_shape=jax.ShapeDtypeStruct((t, n_heads * head_dim), BF16),
        scratch_shapes=[pltpu.VMEM((nh, tq, 1), F32), pltpu.VMEM((nh, tq, head_dim), F32)],
        compiler_params=_cparams(("arbitrary", "arbitrary", "arbitrary")),
        name="sb_attention",
    )(p, p, p)


def _route_top2(logits):
    n_e = logits.shape[1]
    lane = lax.broadcasted_iota(jnp.int32, logits.shape, 1)
    m1 = jnp.max(logits, axis=-1, keepdims=True)
    i1 = jnp.min(jnp.where(logits == m1, lane, n_e), axis=-1, keepdims=True)
    rest = jnp.where(lane == i1, -jnp.inf, logits)
    m2 = jnp.max(rest, axis=-1, keepdims=True)
    i2 = jnp.min(jnp.where(rest == m2, lane, n_e), axis=-1, keepdims=True)
    t = jnp.exp(m2 - m1)
    w1 = 1.0 / (1.0 + t)
    w2 = t / (1.0 + t)
    out = jnp.where(lane == 0, i1.astype(F32), 0.0)
    out = jnp.where(lane == 1, i2.astype(F32), out)
    out = jnp.where(lane == 2, w1, out)
    out = jnp.where(lane == 3, w2, out)
    return out


def _merge_kernel(*refs, chunk, gdim, moe, dw):
    ysb_ref, su_ref, sv_ref = refs[:3]
    ga_refs = refs[3:3 + dw]
    gb_refs = refs[3 + dw:3 + 2 * dw]
    refs = refs[3 + 2 * dw:]
    x_ref, gate1_ref, shift2_ref, scale2_ref, g2_ref, ws_ref, bs_ref, pa_ref, pb_ref, wo_ref = refs[:10]
    if moe:
        wr_ref, xo_ref, h_ref, route_ref, ysg_sc, m_sc = refs[10:]
    else:
        xo_ref, h_ref, ysg_sc, m_sc = refs[10:]
    tm = x_ref.shape[0]
    width = ysb_ref.shape[1]
    n_groups = ws_ref.shape[0]
    row = lax.broadcasted_iota(jnp.int32, (chunk, chunk), 0)
    col = lax.broadcasted_iota(jnp.int32, (chunk, chunk), 1)
    causal = col <= row
    for g in range(n_groups):
        cs = slice(g * gdim, (g + 1) * gdim)
        wg = jnp.where(causal, ws_ref[g], 0.0).astype(BF16)
        for ch in range(tm // chunk):
            rs = slice(ch * chunk, (ch + 1) * chunk)
            mixed = jnp.dot(wg, sv_ref[rs, cs], preferred_element_type=F32) + bs_ref[g]
            ysg_sc[rs, cs] = (su_ref[rs, cs].astype(F32) * mixed).astype(BF16)
    for s in range(dw):
        sl = slice(s * width, (s + 1) * width)
        a = jnp.dot(ysb_ref[...], pa_ref[:, sl], preferred_element_type=F32)
        b = jnp.dot(ysg_sc[...], pb_ref[:, sl], preferred_element_type=F32)
        m_sc[:, sl] = (ga_refs[s][...].astype(F32) * a + gb_refs[s][...].astype(F32) * b).astype(BF16)
    o = jnp.dot(m_sc[...], wo_ref[...], preferred_element_type=F32)
    xn = x_ref[...] + gate1_ref[...] * o
    xo_ref[...] = xn
    y = xn * lax.rsqrt(jnp.mean(xn * xn, axis=-1, keepdims=True) + EPS)
    h = (y * g2_ref[...]) * (1.0 + scale2_ref[...]) + shift2_ref[...]
    h_ref[...] = h.astype(h_ref.dtype)
    if moe:
        wr = wr_ref[...]
        h_hi, w_hi = h.astype(BF16), wr.astype(BF16)
        h_lo = (h - h_hi.astype(F32)).astype(BF16)
        w_lo = (wr - w_hi.astype(F32)).astype(BF16)
        logits = (jnp.dot(h_hi, w_hi, preferred_element_type=F32)
                  + jnp.dot(h_lo, w_hi, preferred_element_type=F32)
                  + jnp.dot(h_hi, w_lo, preferred_element_type=F32))
        route_ref[...] = _route_top2(logits)


def _merge(ysb, p, x2, mod_l, norm2_g, w_s, bs_b, pa_bf, pb_bf, wo_bf, w_router, *, seq, tm, width):
    t, d = x2.shape
    n_groups, chunk, _ = w_s.shape
    gdim = width // n_groups
    tpb = seq // tm
    moe = w_router is not None
    dw = d // width
    vec = lambda which: pl.BlockSpec((None, None, 1, d), lambda i: (i // tpb, which, 0, 0))
    resident = lambda shape: pl.BlockSpec(shape, lambda i: (0,) * len(shape), pipeline_mode=pl.Buffered(1))
    pcol = lambda cb: pl.BlockSpec((tm, width), lambda i: (i, cb))
    in_specs = [
        pl.BlockSpec((tm, width), lambda i: (i, 0)),
        pcol(3),
        pcol(4),
        *[pcol(5 + s) for s in range(dw)],
        *[pcol(5 + dw + s) for s in range(dw)],
        pl.BlockSpec((tm, d), lambda i: (i, 0)),
        vec(2), vec(3), vec(4),
        resident((1, d)),
        resident((n_groups, chunk, chunk)),
        resident((n_groups, chunk, gdim)),
        resident((width, d)), resident((width, d)), resident((d, d)),
    ]
    args = [ysb, p, p, *([p] * (2 * dw)), x2, mod_l, mod_l, mod_l, norm2_g.reshape(1, d), w_s, bs_b,
            pa_bf, pb_bf, wo_bf]
    out_specs = [pl.BlockSpec((tm, d), lambda i: (i, 0)), pl.BlockSpec((tm, d), lambda i: (i, 0))]
    out_shape = [jax.ShapeDtypeStruct((t, d), F32), jax.ShapeDtypeStruct((t, d), F32 if moe else BF16)]
    if moe:
        n_e = w_router.shape[1]
        in_specs.append(resident((d, n_e)))
        args.append(w_router)
        out_specs.append(pl.BlockSpec((tm, n_e), lambda i: (i, 0)))
        out_shape.append(jax.ShapeDtypeStruct((t, n_e), F32))
    kern = functools.partial(_merge_kernel, chunk=chunk, gdim=gdim, moe=moe, dw=dw)
    return pl.pallas_call(
        kern,
        grid=(t // tm,),
        in_specs=in_specs,
        out_specs=out_specs,
        out_shape=out_shape,
        scratch_shapes=[pltpu.VMEM((tm, width), BF16), pltpu.VMEM((tm, d), BF16)],
        compiler_params=_cparams(("arbitrary",)),
        name="merge_moe" if moe else "merge_dense",
    )(*args)


def _swiglu_step(x_ref, w_refs, w_scs, o_ref, n_units, unit):
    w1_sc, w3_sc, w2_sc = w_scs
    big = 4 * unit

    def cast_weights():
        for w_ref, w_sc in zip(w_refs, w_scs):
            w_sc[...] = w_ref[...].astype(BF16)

    def rows(r0, m):
        xs = x_ref[pl.ds(r0, m), :]
        a = jnp.dot(xs, w1_sc[...], preferred_element_type=F32)
        b = jnp.dot(xs, w3_sc[...], preferred_element_type=F32)
        hh = (jax.nn.silu(a) * b).astype(BF16)
        o_ref[pl.ds(r0, m), :] += jnp.dot(hh, w2_sc[...], preferred_element_type=F32)

    if isinstance(n_units, int):
        assert n_units % 4 == 0
        cast_weights()
        for s in range(n_units // 4):
            rows(s * big, big)
        return

    max_units = x_ref.shape[0] // unit
    assert max_units <= 2 * (4 + 1), "two blocks of at most 5 units cover a tile"
    for u in range(1, max_units + 1):
        @pl.when(n_units == u)
        def _(u=u):
            first = u if u <= 4 else -(-u // 2)
            cast_weights()
            rows(0, first * unit)
            if u > first:
                rows(first * unit, (u - first) * unit)


def _ffn_dense_kernel(x_ref, w1_ref, w3_ref, w2_ref, res_ref, gate_ref, o_ref, w1_sc, w3_sc, w2_sc, *, unit):
    j = pl.program_id(1)

    @pl.when(j == 0)
    def _():
        o_ref[...] = jnp.zeros_like(o_ref)

    _swiglu_step(x_ref, (w1_ref, w3_ref, w2_ref), (w1_sc, w3_sc, w2_sc), o_ref, x_ref.shape[0] // unit, unit)

    @pl.when(j == pl.num_programs(1) - 1)
    def _():
        o_ref[...] = res_ref[...] + gate_ref[...] * o_ref[...]


def _ffn_dense(h, w1, w3, w2, res, mod_l, *, seq, tm, tf, unit):
    n, d = h.shape
    n_f = w1.shape[1]
    tpb = seq // tm
    return pl.pallas_call(
        functools.partial(_ffn_dense_kernel, unit=unit),
        grid=(n // tm, n_f // tf),
        in_specs=[
            pl.BlockSpec((tm, d), lambda i, j: (i, 0)),
            pl.BlockSpec((d, tf), lambda i, j: (0, j)),
            pl.BlockSpec((d, tf), lambda i, j: (0, j)),
            pl.BlockSpec((tf, d), lambda i, j: (j, 0)),
            pl.BlockSpec((tm, d), lambda i, j: (i, 0), pipeline_mode=pl.Buffered(1)),
            pl.BlockSpec((None, None, 1, d), lambda i, j: (i // tpb, 5, 0, 0)),
        ],
        out_specs=pl.BlockSpec((tm, d), lambda i, j: (i, 0)),
        out_shape=jax.ShapeDtypeStruct((n, d), F32),
        scratch_shapes=[pltpu.VMEM((d, tf), BF16), pltpu.VMEM((d, tf), BF16), pltpu.VMEM((tf, d), BF16)],
        compiler_params=_cparams(("arbitrary", "arbitrary")),
        name="ffn_dense",
    )(h, w1, w3, w2, res, mod_l)


def _row_copy(src_hbm, src_row, dst_ref, dst_row, sem):
    return pltpu.make_async_copy(src_hbm.at[pl.ds(src_row, 1), :], dst_ref.at[pl.ds(dst_row, 1), :], sem)


def _ffn_moe_kernel(te_ref, tv_ref, src_ref, x_hbm, w1_ref, w3_ref, w2_ref, o_ref, xg_sc, xb_sc, w1_sc, w3_sc,
                    w2_sc, sem, *, unit, rows_per_step):
    del te_ref
    i = pl.program_id(0)
    j = pl.program_id(1)
    tm = xb_sc.shape[0]
    n_units = tv_ref[i]

    def start_row(tile, r):
        _row_copy(x_hbm, src_ref[tile * tm + r], xg_sc, r, sem).start()

    @pl.when(jnp.logical_and(i == 0, j == 0))
    def _():
        lax.fori_loop(0, tv_ref[0] * unit, lambda r, c: (start_row(0, r), c)[1], 0)

    @pl.when(j == 0)
    def _():
        def wait_unit(u, c):
            r0 = pl.multiple_of(u * unit, unit)
            pltpu.make_async_copy(x_hbm.at[pl.ds(0, unit), :], xg_sc.at[pl.ds(r0, unit), :], sem).wait()
            return c

        lax.fori_loop(0, n_units, wait_unit, 0)

        def cast_unit(u, c):
            r0 = pl.multiple_of(u * unit, unit)
            xb_sc[pl.ds(r0, unit), :] = xg_sc[pl.ds(r0, unit), :].astype(BF16)
            return c

        lax.fori_loop(0, n_units, cast_unit, 0)
        o_ref[...] = jnp.zeros_like(o_ref)

    @pl.when(i + 1 < pl.num_programs(0))
    def _():
        n_next = tv_ref[i + 1] * unit
        for g in range(rows_per_step // DMA_GROUP):
            first = j * rows_per_step + g * DMA_GROUP

            @pl.when(first < n_next)
            def _(first=first):
                for r in range(DMA_GROUP):
                    start_row(i + 1, first + r)

    @pl.when(n_units > 0)
    def _():
        _swiglu_step(xb_sc, (w1_ref, w3_ref, w2_ref), (w1_sc, w3_sc, w2_sc), o_ref, n_units, unit)


def _ffn_moe(h, w1, w3, w2, tile_e, tile_v, src, *, tm, tf, unit):
    n_pad = src.shape[0]
    d = h.shape[1]
    n_f = w1.shape[2]
    nj = n_f // tf
    jj = lambda i, j, tv: jnp.where(tv[i] > 0, j, nj - 1)
    rows_per_step = -(-tm // (nj * DMA_GROUP)) * DMA_GROUP
    assert unit % DMA_GROUP == 0 and tm % unit == 0
    return pl.pallas_call(
        functools.partial(_ffn_moe_kernel, unit=unit, rows_per_step=rows_per_step),
        grid_spec=pltpu.PrefetchScalarGridSpec(
            num_scalar_prefetch=3,
            grid=(n_pad // tm, nj),
            in_specs=[
                pl.BlockSpec(memory_space=pl.ANY),
                pl.BlockSpec((None, d, tf), lambda i, j, te, tv, src: (te[i], 0, jj(i, j, tv))),
                pl.BlockSpec((None, d, tf), lambda i, j, te, tv, src: (te[i], 0, jj(i, j, tv))),
                pl.BlockSpec((None, tf, d), lambda i, j, te, tv, src: (te[i], jj(i, j, tv), 0)),
            ],
            out_specs=pl.BlockSpec((tm, d), lambda i, j, te, tv, src: (i, 0)),
            scratch_shapes=[pltpu.VMEM((tm, d), h.dtype), pltpu.VMEM((tm, d), BF16),
                            pltpu.VMEM((d, tf), BF16), pltpu.VMEM((d, tf), BF16), pltpu.VMEM((tf, d), BF16),
                            pltpu.SemaphoreType.DMA(())],
        ),
        out_shape=jax.ShapeDtypeStruct((n_pad, d), F32),
        compiler_params=_cparams(("arbitrary", "arbitrary")),
        name="ffn_moe",
    )(tile_e, tile_v, src, h, w1, w3, w2)


def _combine_kernel(pos0_ref, pos1_ref, ys_hbm, x_ref, gate_ref, w0_ref, w1_ref, o_ref, buf0, buf1, sem):
    tc = x_ref.shape[0]
    i = pl.program_id(0)

    def fetch(step):
        slot = step % 2
        base = step * tc

        def issue(g, c):
            for u in range(DMA_GROUP):
                r = g * DMA_GROUP + u
                _row_copy(ys_hbm, pos0_ref[base + r], buf0.at[slot], r, sem.at[slot]).start()
                _row_copy(ys_hbm, pos1_ref[base + r], buf1.at[slot], r, sem.at[slot]).start()
            return c

        lax.fori_loop(0, tc // DMA_GROUP, issue, 0)

    @pl.when(i == 0)
    def _():
        fetch(0)

    @pl.when(i + 1 < pl.num_programs(0))
    def _():
        fetch(i + 1)

    slot = i % 2
    for buf in (buf0, buf1):
        pltpu.make_async_copy(ys_hbm.at[pl.ds(0, tc), :], buf.at[slot], sem.at[slot]).wait()
    o_ref[...] = x_ref[...] + gate_ref[...] * (w0_ref[...] * buf0[slot] + w1_ref[...] * buf1[slot])


def _combine(pos0, pos1, ys, x2, mod_l, w0, w1, *, seq, tc):
    t, d = x2.shape
    tpb = seq // tc
    return pl.pallas_call(
        _combine_kernel,
        grid_spec=pltpu.PrefetchScalarGridSpec(
            num_scalar_prefetch=2,
            grid=(t // tc,),
            in_specs=[pl.BlockSpec(memory_space=pl.ANY),
                      pl.BlockSpec((tc, d), lambda i, p0, p1: (i, 0)),
                      pl.BlockSpec((None, None, 1, d), lambda i, p0, p1: (i // tpb, 5, 0, 0)),
                      pl.BlockSpec((tc, 1), lambda i, p0, p1: (i, 0)),
                      pl.BlockSpec((tc, 1), lambda i, p0, p1: (i, 0))],
            out_specs=pl.BlockSpec((tc, d), lambda i, p0, p1: (i, 0)),
            scratch_shapes=[pltpu.VMEM((2, tc, d), F32), pltpu.VMEM((2, tc, d), F32),
                            pltpu.SemaphoreType.DMA((2,))],
        ),
        out_shape=jax.ShapeDtypeStruct((t, d), F32),
        compiler_params=_cparams(("arbitrary",)),
        name="moe_combine",
    )(pos0, pos1, ys, x2, mod_l, w0, w1)


def _route_metadata(route, n_experts, tm, unit):
    t = route.shape[0]
    e = route[:, :TOP_K].astype(jnp.int32)
    w = route[:, TOP_K:2 * TOP_K]
    cnt = jnp.sum((e[:, :, None] == jnp.arange(n_experts, dtype=jnp.int32)).astype(jnp.int32), axis=1)
    csum = jnp.cumsum(cnt, axis=0)
    rank = csum - cnt
    counts = csum[-1]
    padded = ((counts + tm - 1) // tm) * tm
    ends = jnp.cumsum(padded)
    starts = ends - padded
    pos = starts[e] + jnp.take_along_axis(rank, e, axis=1)
    n_tiles = (TOP_K * t + n_experts * (tm - 1)) // tm
    n_pad = n_tiles * tm
    tok = jnp.broadcast_to(jnp.arange(t, dtype=jnp.int32)[:, None], (t, TOP_K))
    flat = pos.reshape(-1)
    src = jnp.zeros((n_pad,), jnp.int32).at[flat].set(tok.reshape(-1), unique_indices=True)
    tile_start = jnp.arange(n_tiles, dtype=jnp.int32) * tm
    tile_e = jnp.sum((tile_start[:, None] >= ends[None, :]).astype(jnp.int32), axis=1)
    tile_e = jnp.minimum(tile_e, n_experts - 1)
    real_rows = jnp.clip(starts[tile_e] + counts[tile_e] - tile_start, 0, tm)
    tile_v = ((real_rows + unit - 1) // unit).astype(jnp.int32)
    return pos[:, 0], pos[:, 1], w[:, 0:1], w[:, 1:2], src, tile_e, tile_v


def kernel(x, c, w_ada, b_ada, norm1_g, norm2_g, w_in, q_norm_g, k_norm_g, sg_norm_g, w_s, b_s, w_proj_sb,
           w_proj_sg, w_out, ffn_w1, ffn_w3, ffn_w2, moe_router, moe_w1, moe_w3, moe_w2):
    batch, seq, d = x.shape
    depth = w_ada.shape[0]
    head_dim = q_norm_g.shape[1]
    width = w_proj_sb.shape[1]
    n_heads = width // head_dim
    n_groups, chunk = w_s.shape[1], w_s.shape[2]
    gdim = width // n_groups
    n_experts = moe_router.shape[2]
    assert sg_norm_g.shape[1] == width and d % width == 0 and w_in.shape[2] == 5 * width + 2 * d
    assert head_dim % LANES == 0 and gdim % LANES == 0 and n_experts >= 2 * TOP_K
    cfg = _tiles(seq, d)
    t = batch * seq

    mod = _adaln(c, w_ada, b_ada, cfg["ada_tn"]).reshape(depth, batch, N_MOD, 1, d)
    x2 = x.reshape(t, d)
    for i in range(depth):
        mod_l = mod[i]
        p = _in_proj(x2, mod_l, norm1_g[i], w_in, i, q_norm_g[i], k_norm_g[i], sg_norm_g[i],
                     seq=seq, tm=cfg["in_tm"], width=width, head_dim=head_dim)
        ysb = _sb_attention(p, batch=batch, seq=seq, n_heads=n_heads, head_dim=head_dim, tq=cfg["attn_tq"])
        bs_b = jnp.broadcast_to(b_s[i][:, :, None], (n_groups, chunk, gdim))
        moe = i % 2 == 1
        outs = _merge(ysb, p, x2, mod_l, norm2_g[i], w_s[i], bs_b, w_proj_sb[i].astype(BF16),
                      w_proj_sg[i].astype(BF16), w_out[i].astype(BF16), moe_router[i // 2] if moe else None,
                      seq=seq, tm=cfg["merge_tm"], width=width)
        if not moe:
            x2, h = outs
            jd = i // 2
            x2 = _ffn_dense(h, ffn_w1[jd], ffn_w3[jd], ffn_w2[jd], x2, mod_l, seq=seq,
                            tm=cfg["ffn_tm"], tf=cfg["ffn_tf"], unit=cfg["ffn_unit"])
        else:
            x2, h, route = outs
            jm = i // 2
            pos0, pos1, w0, w1, src, tile_e, tile_v = _route_metadata(route, n_experts, cfg["moe_tm"],
                                                                      cfg["ffn_unit"])
            ys = _ffn_moe(h, moe_w1[jm], moe_w3[jm], moe_w2[jm], tile_e, tile_v, src,
                          tm=cfg["moe_tm"], tf=cfg["ffn_tf"], unit=cfg["ffn_unit"])
            x2 = _combine(pos0, pos1, ys, x2, mod_l, w0, w1, seq=seq, tc=cfg["gather_rows"])
    return x2.reshape(batch, seq, d)
```

```python
import functools

import jax
import jax.numpy as jnp
from jax import lax
from jax.experimental import pallas as pl
from jax.experimental.pallas import tpu as pltpu

F32 = jnp.float32
BF16 = jnp.bfloat16
EPS = 1e-6
N_MOD = 6
TOP_K = 2
LOG2E = 1.4426950408889634
ATTN_UNROLL = 4
ATTN_HEADS = 2
LANES = 128
MXU_N = 256
DMA_GROUP = 8
FFN_BLOCK = 512
VMEM_LIMIT = 56 * 1024 * 1024


def _tiles(seq, d_model):
    return dict(
        ada_tn=min(1024, d_model),
        in_tm=min(1024, seq),
        attn_tq=min(256, seq),
        merge_tm=min(256, seq),
        ffn_tm=min(1024, seq),
        ffn_unit=min(128, seq // 4),
        moe_tm=min(1024, seq) + min(128, seq // 4),
        ffn_tf=256,
        gather_rows=min(256, seq),
    )


def _cparams(sem):
    return pltpu.CompilerParams(dimension_semantics=sem, vmem_limit_bytes=VMEM_LIMIT)


def _adaln_kernel(cb_ref, w_ref, b_ref, o_ref):
    tn = w_ref.shape[1]
    for b in range(cb_ref.shape[0]):
        cb = cb_ref[b]
        act = cb * jax.nn.sigmoid(cb)
        for j in range(tn // LANES):
            sl = slice(j * LANES, (j + 1) * LANES)
            s = jnp.sum(w_ref[:, sl] * act, axis=0, keepdims=True)
            o_ref[b:b + 1, sl] = s + b_ref[:, sl]


def _adaln(c, w_ada, b_ada, tn):
    depth, k, n = w_ada.shape
    nb = c.shape[0]
    cb = jnp.broadcast_to(c[:, :, None], (nb, k, LANES))
    return pl.pallas_call(
        _adaln_kernel,
        grid=(depth, n // tn),
        in_specs=[
            pl.BlockSpec((nb, k, LANES), lambda l, j: (0, 0, 0)),
            pl.BlockSpec((None, k, tn), lambda l, j: (l, 0, j)),
            pl.BlockSpec((None, 1, tn), lambda l, j: (l, 0, j)),
        ],
        out_specs=pl.BlockSpec((None, nb, tn), lambda l, j: (l, 0, j)),
        out_shape=jax.ShapeDtypeStruct((depth, nb, n), F32),
        compiler_params=_cparams(("arbitrary", "arbitrary")),
        name="adaln",
    )(cb, w_ada, b_ada.reshape(depth, 1, n))


def _in_proj_kernel(x_ref, shift_ref, scale_ref, g_ref, w_ref, qg_ref, kg_ref, sgg_ref, o_ref, h_sc, gv_sc,
                    *, head_dim, qk_scale):
    j = pl.program_id(1)

    @pl.when(j == 0)
    def _():
        x = x_ref[...]
        y = x * lax.rsqrt(jnp.mean(x * x, axis=-1, keepdims=True) + EPS)
        h_sc[...] = ((y * g_ref[...]) * (1.0 + scale_ref[...]) + shift_ref[...]).astype(BF16)

    tn = o_ref.shape[1]

    def project(epilogue):
        for c in range(tn // MXU_N):
            sl = slice(c * MXU_N, (c + 1) * MXU_N)
            y = jnp.dot(h_sc[...], w_ref[:, sl].astype(BF16), preferred_element_type=F32)
            epilogue(y, c * MXU_N)

    def head_norm(gain):
        def epilogue(y, c0):
            for h in range(MXU_N // head_dim):
                yh = y[:, h * head_dim:(h + 1) * head_dim]
                r = lax.rsqrt(jnp.mean(yh * yh, axis=-1, keepdims=True) + EPS)
                o_ref[:, c0 + h * head_dim:c0 + (h + 1) * head_dim] = ((yh * r) * gain).astype(BF16)
        return epilogue

    def elementwise(fn):
        def epilogue(y, c0):
            o_ref[:, c0:c0 + MXU_N] = fn(y).astype(BF16)
        return epilogue

    @pl.when(j == 0)
    def _():
        project(head_norm(qg_ref[...] * qk_scale))

    @pl.when(j == 1)
    def _():
        project(head_norm(kg_ref[...]))

    @pl.when(j == 2)
    def _():
        project(elementwise(lambda y: y))

    @pl.when(j == 3)
    def _():
        project(elementwise(jax.nn.gelu))

    @pl.when(j == 4)
    def _():
        ssq = []

        def epilogue(y, c0):
            gv = jax.nn.gelu(y)
            gv_sc[:, c0:c0 + MXU_N] = gv
            ssq.append(jnp.sum(gv * gv, axis=-1, keepdims=True))

        project(epilogue)
        r = lax.rsqrt(sum(ssq) / tn + EPS)
        o_ref[...] = ((gv_sc[...] * r) * sgg_ref[...]).astype(BF16)

    @pl.when(j >= 5)
    def _():
        project(elementwise(lambda y: 0.5 * jnp.tanh(0.5 * y) + 0.5))


def _in_proj(x2, mod_l, norm_g, w_in, layer, q_g, k_g, sg_g, *, seq, tm, width, head_dim):
    t, d = x2.shape
    n = w_in.shape[2]
    tpb = seq // tm
    vec = lambda which: pl.BlockSpec((None, None, 1, d), lambda i, j: (i // tpb, which, 0, 0))
    const = lambda shape: pl.BlockSpec(shape, lambda i, j: (0,) * len(shape))
    kern = functools.partial(_in_proj_kernel, head_dim=head_dim, qk_scale=float(head_dim) ** -0.5 * LOG2E)
    return pl.pallas_call(
        kern,
        grid=(t // tm, n // width),
        in_specs=[
            pl.BlockSpec((tm, d), lambda i, j: (i, 0), pipeline_mode=pl.Buffered(1)),
            vec(0), vec(1),
            const((1, d)),
            pl.BlockSpec((None, d, width), lambda i, j: (layer, 0, j)),
            const((1, head_dim)), const((1, head_dim)), const((1, width)),
        ],
        out_specs=pl.BlockSpec((tm, width), lambda i, j: (i, j)),
        out_shape=jax.ShapeDtypeStruct((t, n), BF16),
        scratch_shapes=[pltpu.VMEM((tm, d), BF16), pltpu.VMEM((tm, width), F32)],
        compiler_params=_cparams(("arbitrary", "arbitrary")),
        name="in_proj",
    )(x2, mod_l, mod_l, norm_g.reshape(1, d), w_in, q_g.reshape(1, head_dim), k_g.reshape(1, head_dim),
      sg_g.reshape(1, width))


def _sb_attn_kernel(q_ref, k_ref, v_ref, o_ref, carry_sc, acc_sc, *, unroll, dh):
    i = pl.program_id(2)
    tq = q_ref.shape[0]
    heads = range(q_ref.shape[1] // dh)
    hs = lambda h: slice(h * dh, (h + 1) * dh)
    qs = [q_ref[:, hs(h)] for h in heads]
    row = lax.broadcasted_iota(jnp.int32, (tq, tq), 0)
    col = lax.broadcasted_iota(jnp.int32, (tq, tq), 1)
    suffix = (row >= col).astype(BF16)
    strict = col < row

    def scores(kb, h, diagonal):
        start = pl.multiple_of(kb * tq, tq)
        z = lax.dot_general(qs[h], k_ref[pl.ds(start, tq), hs(h)], (((1,), (1,)), ((), ())),
                            preferred_element_type=F32)
        sp = jnp.maximum(z, 0.0) + jnp.log2(1.0 + jnp.exp2(-jnp.abs(z)))
        if diagonal:
            sp = jnp.where(strict, sp, 0.0)
        incl = jnp.dot(sp.astype(BF16), suffix, preferred_element_type=F32)
        return start, h, z, incl, diagonal

    def accumulate(parts, state):
        carry = [state[2 * h] for h in heads]
        acc = [state[2 * h + 1] for h in heads]
        for start, h, z, incl, diagonal in parts:
            w = jnp.exp2(z - incl - carry[h])
            if diagonal:
                w = jnp.where(strict, w, 0.0)
            acc[h] = acc[h] + jnp.dot(w.astype(BF16), v_ref[pl.ds(start, tq), hs(h)],
                                      preferred_element_type=F32)
            carry[h] = carry[h] + incl[:, 0:1]
        return tuple(x for h in heads for x in (carry[h], acc[h]))

    rem = i % unroll
    for r in range(unroll):
        @pl.when(rem == r)
        def _(r=r):
            parts = [scores(i, h, True) for h in heads]
            parts += [scores(i - 1 - u, h, False) for u in range(r) for h in heads]
            zero = tuple(x for h in heads for x in (jnp.zeros((tq, 1), F32), jnp.zeros((tq, dh), F32)))
            state = accumulate(parts, zero)
            for h in heads:
                carry_sc[h] = state[2 * h]
                acc_sc[h] = state[2 * h + 1]

    def group(s, state):
        first = i - rem - 1 - s * unroll
        return accumulate([scores(first - u, h, False) for u in range(unroll) for h in heads], state)

    state = tuple(x for h in heads for x in (carry_sc[h], acc_sc[h]))
    state = lax.fori_loop(0, i // unroll, group, state)
    for h in heads:
        o_ref[:, hs(h)] = state[2 * h + 1].astype(BF16)


def _sb_attention(p, *, batch, seq, n_heads, head_dim, tq):
    t = p.shape[0]
    nq = seq // tq
    nh = ATTN_HEADS
    assert n_heads % nh == 0
    ng = n_heads // nh
    wide = nh * head_dim
    return pl.pallas_call(
        functools.partial(_sb_attn_kernel, unroll=ATTN_UNROLL, dh=head_dim),
        grid=(batch, ng, nq),
        in_specs=[
            pl.BlockSpec((tq, wide), lambda b, g, i: (b * nq + i, g)),
            pl.BlockSpec((seq, wide), lambda b, g, i: (b, ng + g)),
            pl.BlockSpec((seq, wide), lambda b, g, i: (b, 2 * ng + g)),
        ],
        out_specs=pl.BlockSpec((tq, wide), lambda b, g, i: (b * nq + i, g)),
        out_shape=jax.ShapeDtypeStruct((t, n_heads * head_dim), BF16),
        scratch_shapes=[pltpu.VMEM((nh, tq, 1), F32), pltpu.VMEM((nh, tq, head_dim), F32)],
        compiler_params=_cparams(("arbitrary", "arbitrary", "arbitrary")),
        name="sb_attention",
    )(p, p, p)


def _route_top2(logits):
    n_e = logits.shape[1]
    lane = lax.broadcasted_iota(jnp.int32, logits.shape, 1)
    m1 = jnp.max(logits, axis=-1, keepdims=True)
    i1 = jnp.min(jnp.where(logits == m1, lane, n_e), axis=-1, keepdims=True)
    rest = jnp.where(lane == i1, -jnp.inf, logits)
    m2 = jnp.max(rest, axis=-1, keepdims=True)
    i2 = jnp.min(jnp.where(rest == m2, lane, n_e), axis=-1, keepdims=True)
    t = jnp.exp(m2 - m1)
    w1 = 1.0 / (1.0 + t)
    w2 = t / (1.0 + t)
    out = jnp.where(lane == 0, i1.astype(F32), 0.0)
    out = jnp.where(lane == 1, i2.astype(F32), out)
    out = jnp.where(lane == 2, w1, out)
    out = jnp.where(lane == 3, w2, out)
    return out


def _merge_kernel(*refs, chunk, gdim, moe, dw):
    ysb_ref, su_ref, sv_ref = refs[:3]
    ga_refs = refs[3:3 + dw]
    gb_refs = refs[3 + dw:3 + 2 * dw]
    refs = refs[3 + 2 * dw:]
    x_ref, gate1_ref, shift2_ref, scale2_ref, g2_ref, ws_ref, bs_ref, pa_ref, pb_ref, wo_ref = refs[:10]
    if moe:
        wr_ref, xo_ref, h_ref, route_ref, ysg_sc, m_sc = refs[10:]
    else:
        xo_ref, h_ref, ysg_sc, m_sc = refs[10:]
    tm = x_ref.shape[0]
    width = ysb_ref.shape[1]
    n_groups = ws_ref.shape[0]
    row = lax.broadcasted_iota(jnp.int32, (chunk, chunk), 0)
    col = lax.broadcasted_iota(jnp.int32, (chunk, chunk), 1)
    causal = col <= row
    for g in range(n_groups):
        cs = slice(g * gdim, (g + 1) * gdim)
        wg = jnp.where(causal, ws_ref[g], 0.0).astype(BF16)
        for ch in range(tm // chunk):
            rs = slice(ch * chunk, (ch + 1) * chunk)
            mixed = jnp.dot(wg, sv_ref[rs, cs], preferred_element_type=F32) + bs_ref[g]
            ysg_sc[rs, cs] = (su_ref[rs, cs].astype(F32) * mixed).astype(BF16)
    for s in range(dw):
        sl = slice(s * width, (s + 1) * width)
        a = jnp.dot(ysb_ref[...], pa_ref[:, sl], preferred_element_type=F32)
        b = jnp.dot(ysg_sc[...], pb_ref[:, sl], preferred_element_type=F32)
        m_sc[:, sl] = (ga_refs[s][...].astype(F32) * a + gb_refs[s][...].astype(F32) * b).astype(BF16)
    o = jnp.dot(m_sc[...], wo_ref[...], preferred_element_type=F32)
    xn = x_ref[...] + gate1_ref[...] * o
    xo_ref[...] = xn
    y = xn * lax.rsqrt(jnp.mean(xn * xn, axis=-1, keepdims=True) + EPS)
    h = (y * g2_ref[...]) * (1.0 + scale2_ref[...]) + shift2_ref[...]
    h_ref[...] = h.astype(h_ref.dtype)
    if moe:
        wr = wr_ref[...]
        h_hi, w_hi = h.astype(BF16), wr.astype(BF16)
        h_lo = (h - h_hi.astype(F32)).astype(BF16)
        w_lo = (wr - w_hi.astype(F32)).astype(BF16)
        logits = (jnp.dot(h_hi, w_hi, preferred_element_type=F32)
                  + jnp.dot(h_lo, w_hi, preferred_element_type=F32)
                  + jnp.dot(h_hi, w_lo, preferred_element_type=F32))
        route_ref[...] = _route_top2(logits)


def _merge(ysb, p, x2, mod_l, norm2_g, w_s, bs_b, pa_bf, pb_bf, wo_bf, w_router, *, seq, tm, width):
    t, d = x2.shape
    n_groups, chunk, _ = w_s.shape
    gdim = width // n_groups
    tpb = seq // tm
    moe = w_router is not None
    dw = d // width
    vec = lambda which: pl.BlockSpec((None, None, 1, d), lambda i: (i // tpb, which, 0, 0))
    resident = lambda shape: pl.BlockSpec(shape, lambda i: (0,) * len(shape), pipeline_mode=pl.Buffered(1))
    pcol = lambda cb: pl.BlockSpec((tm, width), lambda i: (i, cb))
    in_specs = [
        pl.BlockSpec((tm, width), lambda i: (i, 0)),
        pcol(3),
        pcol(4),
        *[pcol(5 + s) for s in range(dw)],
        *[pcol(5 + dw + s) for s in range(dw)],
        pl.BlockSpec((tm, d), lambda i: (i, 0)),
        vec(2), vec(3), vec(4),
        resident((1, d)),
        resident((n_groups, chunk, chunk)),
        resident((n_groups, chunk, gdim)),
        resident((width, d)), resident((width, d)), resident((d, d)),
    ]
    args = [ysb, p, p, *([p] * (2 * dw)), x2, mod_l, mod_l, mod_l, norm2_g.reshape(1, d), w_s, bs_b,
            pa_bf, pb_bf, wo_bf]
    out_specs = [pl.BlockSpec((tm, d), lambda i: (i, 0)), pl.BlockSpec((tm, d), lambda i: (i, 0))]
    out_shape = [jax.ShapeDtypeStruct((t, d), F32), jax.ShapeDtypeStruct((t, d), F32 if moe else BF16)]
    if moe:
        n_e = w_router.shape[1]
        in_specs.append(resident((d, n_e)))
        args.append(w_router)
        out_specs.append(pl.BlockSpec((tm, n_e), lambda i: (i, 0)))
        out_shape.append(jax.ShapeDtypeStruct((t, n_e), F32))
    kern = functools.partial(_merge_kernel, chunk=chunk, gdim=gdim, moe=moe, dw=dw)
    return pl.pallas_call(
        kern,
        grid=(t // tm,),
        in_specs=in_specs,
        out_specs=out_specs,
        out_shape=out_shape,
        scratch_shapes=[pltpu.VMEM((tm, width), BF16), pltpu.VMEM((tm, d), BF16)],
        compiler_params=_cparams(("arbitrary",)),
        name="merge_moe" if moe else "merge_dense",
    )(*args)


def _swiglu_step(x_ref, w_refs, w_scs, o_ref, n_units, unit):
    w1_sc, w3_sc, w2_sc = w_scs

    def cast_weights():
        for w_ref, w_sc in zip(w_refs, w_scs):
            w_sc[...] = w_ref[...].astype(BF16)

    def rows(r0, m):
        xs = x_ref[pl.ds(r0, m), :]
        a = jnp.dot(xs, w1_sc[...], preferred_element_type=F32)
        b = jnp.dot(xs, w3_sc[...], preferred_element_type=F32)
        hh = (jax.nn.silu(a) * b).astype(BF16)
        o_ref[pl.ds(r0, m), :] += jnp.dot(hh, w2_sc[...], preferred_element_type=F32)

    if isinstance(n_units, int):
        total = n_units * unit
        block = min(FFN_BLOCK, total)
        assert total % block == 0
        cast_weights()
        for s in range(total // block):
            rows(s * block, block)
        return

    max_units = x_ref.shape[0] // unit
    assert max_units * unit <= 2 * FFN_BLOCK + 2 * LANES, "two blocks of little more than FFN_BLOCK rows cover a tile"
    for u in range(1, max_units + 1):
        @pl.when(n_units == u)
        def _(u=u):
            first = u if u * unit <= FFN_BLOCK else -(-u // 2)
            cast_weights()
            rows(0, first * unit)
            if u > first:
                rows(first * unit, (u - first) * unit)


def _ffn_dense_kernel(x_ref, w1_ref, w3_ref, w2_ref, res_ref, gate_ref, o_ref, w1_sc, w3_sc, w2_sc, *, unit):
    j = pl.program_id(1)

    @pl.when(j == 0)
    def _():
        o_ref[...] = jnp.zeros_like(o_ref)

    _swiglu_step(x_ref, (w1_ref, w3_ref, w2_ref), (w1_sc, w3_sc, w2_sc), o_ref, x_ref.shape[0] // unit, unit)

    @pl.when(j == pl.num_programs(1) - 1)
    def _():
        o_ref[...] = res_ref[...] + gate_ref[...] * o_ref[...]


def _ffn_dense(h, w1, w3, w2, res, mod_l, *, seq, tm, tf, unit):
    n, d = h.shape
    n_f = w1.shape[1]
    tpb = seq // tm
    return pl.pallas_call(
        functools.partial(_ffn_dense_kernel, unit=unit),
        grid=(n // tm, n_f // tf),
        in_specs=[
            pl.BlockSpec((tm, d), lambda i, j: (i, 0)),
            pl.BlockSpec((d, tf), lambda i, j: (0, j)),
            pl.BlockSpec((d, tf), lambda i, j: (0, j)),
            pl.BlockSpec((tf, d), lambda i, j: (j, 0)),
            pl.BlockSpec((tm, d), lambda i, j: (i, 0), pipeline_mode=pl.Buffered(1)),
            pl.BlockSpec((None, None, 1, d), lambda i, j: (i // tpb, 5, 0, 0)),
        ],
        out_specs=pl.BlockSpec((tm, d), lambda i, j: (i, 0)),
        out_shape=jax.ShapeDtypeStruct((n, d), F32),
        scratch_shapes=[pltpu.VMEM((d, tf), BF16), pltpu.VMEM((d, tf), BF16), pltpu.VMEM((tf, d), BF16)],
        compiler_params=_cparams(("arbitrary", "arbitrary")),
        name="ffn_dense",
    )(h, w1, w3, w2, res, mod_l)


def _row_copy(src_hbm, src_row, dst_ref, dst_row, sem):
    return pltpu.make_async_copy(src_hbm.at[pl.ds(src_row, 1), :], dst_ref.at[pl.ds(dst_row, 1), :], sem)


def _ffn_moe_kernel(te_ref, tv_ref, src_ref, x_hbm, w1_ref, w3_ref, w2_ref, o_ref, xg_sc, xb_sc, w1_sc, w3_sc,
                    w2_sc, sem, *, unit, rows_per_step):
    del te_ref
    i = pl.program_id(0)
    j = pl.program_id(1)
    tm = xb_sc.shape[0]
    n_units = tv_ref[i]

    def start_row(tile, r):
        _row_copy(x_hbm, src_ref[tile * tm + r], xg_sc, r, sem).start()

    @pl.when(jnp.logical_and(i == 0, j == 0))
    def _():
        lax.fori_loop(0, tv_ref[0] * unit, lambda r, c: (start_row(0, r), c)[1], 0)

    @pl.when(j == 0)
    def _():
        def wait_unit(u, c):
            r0 = pl.multiple_of(u * unit, unit)
            pltpu.make_async_copy(x_hbm.at[pl.ds(0, unit), :], xg_sc.at[pl.ds(r0, unit), :], sem).wait()
            return c

        lax.fori_loop(0, n_units, wait_unit, 0)

        def cast_unit(u, c):
            r0 = pl.multiple_of(u * unit, unit)
            xb_sc[pl.ds(r0, unit), :] = xg_sc[pl.ds(r0, unit), :].astype(BF16)
            return c

        lax.fori_loop(0, n_units, cast_unit, 0)
        o_ref[...] = jnp.zeros_like(o_ref)

    @pl.when(i + 1 < pl.num_programs(0))
    def _():
        n_next = tv_ref[i + 1] * unit
        for g in range(rows_per_step // DMA_GROUP):
            first = j * rows_per_step + g * DMA_GROUP

            @pl.when(first < n_next)
            def _(first=first):
                for r in range(DMA_GROUP):
                    start_row(i + 1, first + r)

    @pl.when(n_units > 0)
    def _():
        _swiglu_step(xb_sc, (w1_ref, w3_ref, w2_ref), (w1_sc, w3_sc, w2_sc), o_ref, n_units, unit)


def _ffn_moe(h, w1, w3, w2, tile_e, tile_v, src, *, tm, tf, unit):
    n_pad = src.shape[0]
    d = h.shape[1]
    n_f = w1.shape[2]
    nj = n_f // tf
    jj = lambda i, j, tv: jnp.where(tv[i] > 0, j, nj - 1)
    rows_per_step = -(-tm // (nj * DMA_GROUP)) * DMA_GROUP
    assert unit % DMA_GROUP == 0 and tm % unit == 0
    return pl.pallas_call(
        functools.partial(_ffn_moe_kernel, unit=unit, rows_per_step=rows_per_step),
        grid_spec=pltpu.PrefetchScalarGridSpec(
            num_scalar_prefetch=3,
            grid=(n_pad // tm, nj),
            in_specs=[
                pl.BlockSpec(memory_space=pl.ANY),
                pl.BlockSpec((None, d, tf), lambda i, j, te, tv, src: (te[i], 0, jj(i, j, tv))),
                pl.BlockSpec((None, d, tf), lambda i, j, te, tv, src: (te[i], 0, jj(i, j, tv))),
                pl.BlockSpec((None, tf, d), lambda i, j, te, tv, src: (te[i], jj(i, j, tv), 0)),
            ],
            out_specs=pl.BlockSpec((tm, d), lambda i, j, te, tv, src: (i, 0)),
            scratch_shapes=[pltpu.VMEM((tm, d), h.dtype), pltpu.VMEM((tm, d), BF16),
                            pltpu.VMEM((d, tf), BF16), pltpu.VMEM((d, tf), BF16), pltpu.VMEM((tf, d), BF16),
                            pltpu.SemaphoreType.DMA(())],
        ),
        out_shape=jax.ShapeDtypeStruct((n_pad, d), F32),
        compiler_params=_cparams(("arbitrary", "arbitrary")),
        name="ffn_moe",
    )(tile_e, tile_v, src, h, w1, w3, w2)


def _combine_kernel(pos0_ref, pos1_ref, ys_hbm, x_ref, gate_ref, w0_ref, w1_ref, o_ref, buf0, buf1, sem):
    tc = x_ref.shape[0]
    i = pl.program_id(0)

    def fetch(step):
        slot = step % 2
        base = step * tc

        def issue(g, c):
            for u in range(DMA_GROUP):
                r = g * DMA_GROUP + u
                _row_copy(ys_hbm, pos0_ref[base + r], buf0.at[slot], r, sem.at[slot]).start()
                _row_copy(ys_hbm, pos1_ref[base + r], buf1.at[slot], r, sem.at[slot]).start()
            return c

        lax.fori_loop(0, tc // DMA_GROUP, issue, 0)

    @pl.when(i == 0)
    def _():
        fetch(0)

    @pl.when(i + 1 < pl.num_programs(0))
    def _():
        fetch(i + 1)

    slot = i % 2
    for buf in (buf0, buf1):
        pltpu.make_async_copy(ys_hbm.at[pl.ds(0, tc), :], buf.at[slot], sem.at[slot]).wait()
    o_ref[...] = x_ref[...] + gate_ref[...] * (w0_ref[...] * buf0[slot] + w1_ref[...] * buf1[slot])


def _combine(pos0, pos1, ys, x2, mod_l, w0, w1, *, seq, tc):
    t, d = x2.shape
    tpb = seq // tc
    return pl.pallas_call(
        _combine_kernel,
        grid_spec=pltpu.PrefetchScalarGridSpec(
            num_scalar_prefetch=2,
            grid=(t // tc,),
            in_specs=[pl.BlockSpec(memory_space=pl.ANY),
                      pl.BlockSpec((tc, d), lambda i, p0, p1: (i, 0)),
                      pl.BlockSpec((None, None, 1, d), lambda i, p0, p1: (i // tpb, 5, 0, 0)),
                      pl.BlockSpec((tc, 1), lambda i, p0, p1: (i, 0)),
                      pl.BlockSpec((tc, 1), lambda i, p0, p1: (i, 0))],
            out_specs=pl.BlockSpec((tc, d), lambda i, p0, p1: (i, 0)),
            scratch_shapes=[pltpu.VMEM((2, tc, d), F32), pltpu.VMEM((2, tc, d), F32),
                            pltpu.SemaphoreType.DMA((2,))],
        ),
        out_shape=jax.ShapeDtypeStruct((t, d), F32),
        compiler_params=_cparams(("arbitrary",)),
        name="moe_combine",
    )(pos0, pos1, ys, x2, mod_l, w0, w1)


def _route_metadata(route, n_experts, tm, unit):
    t = route.shape[0]
    e = route[:, :TOP_K].astype(jnp.int32)
    w = route[:, TOP_K:2 * TOP_K]
    cnt = jnp.sum((e[:, :, None] == jnp.arange(n_experts, dtype=jnp.int32)).astype(jnp.int32), axis=1)
    csum = jnp.cumsum(cnt, axis=0)
    rank = csum - cnt
    counts = csum[-1]
    padded = ((counts + tm - 1) // tm) * tm
    ends = jnp.cumsum(padded)
    starts = ends - padded
    pos = starts[e] + jnp.take_along_axis(rank, e, axis=1)
    n_tiles = (TOP_K * t + n_experts * (tm - 1)) // tm
    n_pad = n_tiles * tm
    tok = jnp.broadcast_to(jnp.arange(t, dtype=jnp.int32)[:, None], (t, TOP_K))
    flat = pos.reshape(-1)
    src = jnp.zeros((n_pad,), jnp.int32).at[flat].set(tok.reshape(-1), unique_indices=True)
    tile_start = jnp.arange(n_tiles, dtype=jnp.int32) * tm
    tile_e = jnp.sum((tile_start[:, None] >= ends[None, :]).astype(jnp.int32), axis=1)
    tile_e = jnp.minimum(tile_e, n_experts - 1)
    real_rows = jnp.clip(starts[tile_e] + counts[tile_e] - tile_start, 0, tm)
    tile_v = ((real_rows + unit - 1) // unit).astype(jnp.int32)
    return pos[:, 0], pos[:, 1], w[:, 0:1], w[:, 1:2], src, tile_e, tile_v


def kernel(x, c, w_ada, b_ada, norm1_g, norm2_g, w_in, q_norm_g, k_norm_g, sg_norm_g, w_s, b_s, w_proj_sb,
           w_proj_sg, w_out, ffn_w1, ffn_w3, ffn_w2, moe_router, moe_w1, moe_w3, moe_w2):
    batch, seq, d = x.shape
    depth = w_ada.shape[0]
    head_dim = q_norm_g.shape[1]
    width = w_proj_sb.shape[1]
    n_heads = width // head_dim
    n_groups, chunk = w_s.shape[1], w_s.shape[2]
    gdim = width // n_groups
    n_experts = moe_router.shape[2]
    assert sg_norm_g.shape[1] == width and d % width == 0 and w_in.shape[2] == 5 * width + 2 * d
    assert head_dim % LANES == 0 and gdim % LANES == 0 and n_experts >= 2 * TOP_K
    cfg = _tiles(seq, d)
    t = batch * seq

    mod = _adaln(c, w_ada, b_ada, cfg["ada_tn"]).reshape(depth, batch, N_MOD, 1, d)
    x2 = x.reshape(t, d)
    for i in range(depth):
        mod_l = mod[i]
        p = _in_proj(x2, mod_l, norm1_g[i], w_in, i, q_norm_g[i], k_norm_g[i], sg_norm_g[i],
                     seq=seq, tm=cfg["in_tm"], width=width, head_dim=head_dim)
        ysb = _sb_attention(p, batch=batch, seq=seq, n_heads=n_heads, head_dim=head_dim, tq=cfg["attn_tq"])
        bs_b = jnp.broadcast_to(b_s[i][:, :, None], (n_groups, chunk, gdim))
        moe = i % 2 == 1
        outs = _merge(ysb, p, x2, mod_l, norm2_g[i], w_s[i], bs_b, w_proj_sb[i].astype(BF16),
                      w_proj_sg[i].astype(BF16), w_out[i].astype(BF16), moe_router[i // 2] if moe else None,
                      seq=seq, tm=cfg["merge_tm"], width=width)
        if not moe:
            x2, h = outs
            jd = i // 2
            x2 = _ffn_dense(h, ffn_w1[jd], ffn_w3[jd], ffn_w2[jd], x2, mod_l, seq=seq,
                            tm=cfg["ffn_tm"], tf=cfg["ffn_tf"], unit=cfg["ffn_unit"])
        else:
            x2, h, route = outs
            jm = i // 2
            pos0, pos1, w0, w1, src, tile_e, tile_v = _route_metadata(route, n_experts, cfg["moe_tm"],
                                                                      cfg["ffn_unit"])
            ys = _ffn_moe(h, moe_w1[jm], moe_w3[jm], moe_w2[jm], tile_e, tile_v, src,
                          tm=cfg["moe_tm"], tf=cfg["ffn_tf"], unit=cfg["ffn_unit"])
            x2 = _combine(pos0, pos1, ys, x2, mod_l, w0, w1, seq=seq, tc=cfg["gather_rows"])
    return x2.reshape(batch, seq, d)
```

```python
import functools

import jax
import jax.numpy as jnp
from jax import lax
from jax.experimental import pallas as pl
from jax.experimental.pallas import tpu as pltpu

F32 = jnp.float32
BF16 = jnp.bfloat16
EPS = 1e-6
N_MOD = 6
TOP_K = 2
LOG2E = 1.4426950408889634
ATTN_UNROLL = 6
ATTN_HEADS = 2
LANES = 128
MXU_N = 256
DMA_GROUP = 8
FFN_BLOCK = 512
VMEM_LIMIT = 56 * 1024 * 1024


def _tiles(seq, d_model):
    return dict(
        ada_tn=min(1024, d_model),
        in_tm=min(1024, seq),
        attn_tq=min(256, seq),
        merge_tm=min(256, seq),
        ffn_tm=min(1024, seq),
        ffn_unit=min(128, seq // 4),
        moe_tm=min(1024, seq) + min(128, seq // 4),
        ffn_tf=256,
        gather_rows=min(256, seq),
    )


def _cparams(sem):
    return pltpu.CompilerParams(dimension_semantics=sem, vmem_limit_bytes=VMEM_LIMIT)


def _adaln_kernel(cb_ref, w_ref, b_ref, o_ref):
    tn = w_ref.shape[1]
    for b in range(cb_ref.shape[0]):
        cb = cb_ref[b]
        act = cb * jax.nn.sigmoid(cb)
        for j in range(tn // LANES):
            sl = slice(j * LANES, (j + 1) * LANES)
            s = jnp.sum(w_ref[:, sl] * act, axis=0, keepdims=True)
            o_ref[b:b + 1, sl] = s + b_ref[:, sl]


def _adaln(c, w_ada, b_ada, tn):
    depth, k, n = w_ada.shape
    nb = c.shape[0]
    cb = jnp.broadcast_to(c[:, :, None], (nb, k, LANES))
    return pl.pallas_call(
        _adaln_kernel,
        grid=(depth, n // tn),
        in_specs=[
            pl.BlockSpec((nb, k, LANES), lambda l, j: (0, 0, 0)),
            pl.BlockSpec((None, k, tn), lambda l, j: (l, 0, j)),
            pl.BlockSpec((None, 1, tn), lambda l, j: (l, 0, j)),
        ],
        out_specs=pl.BlockSpec((None, nb, tn), lambda l, j: (l, 0, j)),
        out_shape=jax.ShapeDtypeStruct((depth, nb, n), F32),
        compiler_params=_cparams(("arbitrary", "arbitrary")),
        name="adaln",
    )(cb, w_ada, b_ada.reshape(depth, 1, n))


def _in_proj_kernel(x_ref, shift_ref, scale_ref, g_ref, w_ref, qg_ref, kg_ref, sgg_ref, o_ref, h_sc, gv_sc,
                    *, head_dim, qk_scale):
    j = pl.program_id(1)

    @pl.when(j == 0)
    def _():
        x = x_ref[...]
        y = x * lax.rsqrt(jnp.mean(x * x, axis=-1, keepdims=True) + EPS)
        h_sc[...] = ((y * g_ref[...]) * (1.0 + scale_ref[...]) + shift_ref[...]).astype(BF16)

    tn = o_ref.shape[1]

    def project(epilogue):
        for c in range(tn // MXU_N):
            sl = slice(c * MXU_N, (c + 1) * MXU_N)
            y = jnp.dot(h_sc[...], w_ref[:, sl].astype(BF16), preferred_element_type=F32)
            epilogue(y, c * MXU_N)

    def head_norm(gain):
        def epilogue(y, c0):
            for h in range(MXU_N // head_dim):
                yh = y[:, h * head_dim:(h + 1) * head_dim]
                r = lax.rsqrt(jnp.mean(yh * yh, axis=-1, keepdims=True) + EPS)
                o_ref[:, c0 + h * head_dim:c0 + (h + 1) * head_dim] = ((yh * r) * gain).astype(BF16)
        return epilogue

    def elementwise(fn):
        def epilogue(y, c0):
            o_ref[:, c0:c0 + MXU_N] = fn(y).astype(BF16)
        return epilogue

    @pl.when(j == 0)
    def _():
        project(head_norm(qg_ref[...] * qk_scale))

    @pl.when(j == 1)
    def _():
        project(head_norm(kg_ref[...]))

    @pl.when(j == 2)
    def _():
        project(elementwise(lambda y: y))

    @pl.when(j == 3)
    def _():
        project(elementwise(jax.nn.gelu))

    @pl.when(j == 4)
    def _():
        ssq = []

        def epilogue(y, c0):
            gv = jax.nn.gelu(y)
            gv_sc[:, c0:c0 + MXU_N] = gv
            ssq.append(jnp.sum(gv * gv, axis=-1, keepdims=True))

        project(epilogue)
        r = lax.rsqrt(sum(ssq) / tn + EPS)
        o_ref[...] = ((gv_sc[...] * r) * sgg_ref[...]).astype(BF16)

    @pl.when(j >= 5)
    def _():
        project(elementwise(lambda y: 0.5 * jnp.tanh(0.5 * y) + 0.5))


def _in_proj(x2, mod_l, norm_g, w_in, layer, q_g, k_g, sg_g, *, seq, tm, width, head_dim):
    t, d = x2.shape
    n = w_in.shape[2]
    tpb = seq // tm
    vec = lambda which: pl.BlockSpec((None, None, 1, d), lambda i, j: (i // tpb, which, 0, 0))
    const = lambda shape: pl.BlockSpec(shape, lambda i, j: (0,) * len(shape))
    kern = functools.partial(_in_proj_kernel, head_dim=head_dim, qk_scale=float(head_dim) ** -0.5 * LOG2E)
    return pl.pallas_call(
        kern,
        grid=(t // tm, n // width),
        in_specs=[
            pl.BlockSpec((tm, d), lambda i, j: (i, 0), pipeline_mode=pl.Buffered(1)),
            vec(0), vec(1),
            const((1, d)),
            pl.BlockSpec((None, d, width), lambda i, j: (layer, 0, j)),
            const((1, head_dim)), const((1, head_dim)), const((1, width)),
        ],
        out_specs=pl.BlockSpec((tm, width), lambda i, j: (i, j)),
        out_shape=jax.ShapeDtypeStruct((t, n), BF16),
        scratch_shapes=[pltpu.VMEM((tm, d), BF16), pltpu.VMEM((tm, width), F32)],
        compiler_params=_cparams(("arbitrary", "arbitrary")),
        name="in_proj",
    )(x2, mod_l, mod_l, norm_g.reshape(1, d), w_in, q_g.reshape(1, head_dim), k_g.reshape(1, head_dim),
      sg_g.reshape(1, width))


def _sb_attn_kernel(q_ref, k_ref, v_ref, o_ref, carry_sc, acc_sc, *, unroll, dh):
    i = pl.program_id(2)
    tq = q_ref.shape[0]
    heads = range(q_ref.shape[1] // dh)
    hs = lambda h: slice(h * dh, (h + 1) * dh)
    qs = [q_ref[:, hs(h)] for h in heads]
    row = lax.broadcasted_iota(jnp.int32, (tq, tq), 0)
    col = lax.broadcasted_iota(jnp.int32, (tq, tq), 1)
    suffix = (row >= col).astype(BF16)
    strict = col < row

    def scores(kb, h, diagonal):
        start = pl.multiple_of(kb * tq, tq)
        z = lax.dot_general(qs[h], k_ref[pl.ds(start, tq), hs(h)], (((1,), (1,)), ((), ())),
                            preferred_element_type=F32)
        sp = jnp.maximum(z, 0.0) + jnp.log2(1.0 + jnp.exp2(-jnp.abs(z)))
        if diagonal:
            sp = jnp.where(strict, sp, 0.0)
        incl = jnp.dot(sp.astype(BF16), suffix, preferred_element_type=F32)
        return start, h, z, incl, diagonal

    def accumulate(parts, state):
        carry = [state[2 * h] for h in heads]
        acc = [state[2 * h + 1] for h in heads]
        for start, h, z, incl, diagonal in parts:
            w = jnp.exp2(z - incl - carry[h])
            if diagonal:
                w = jnp.where(strict, w, 0.0)
            acc[h] = acc[h] + jnp.dot(w.astype(BF16), v_ref[pl.ds(start, tq), hs(h)],
                                      preferred_element_type=F32)
            carry[h] = carry[h] + incl[:, 0:1]
        return tuple(x for h in heads for x in (carry[h], acc[h]))

    rem = i % unroll
    for r in range(unroll):
        @pl.when(rem == r)
        def _(r=r):
            parts = [scores(i, h, True) for h in heads]
            parts += [scores(i - 1 - u, h, False) for u in range(r) for h in heads]
            zero = tuple(x for h in heads for x in (jnp.zeros((tq, 1), F32), jnp.zeros((tq, dh), F32)))
            state = accumulate(parts, zero)
            for h in heads:
                carry_sc[h] = state[2 * h]
                acc_sc[h] = state[2 * h + 1]

    def group(s, state):
        first = i - rem - 1 - s * unroll
        return accumulate([scores(first - u, h, False) for u in range(unroll) for h in heads], state)

    state = tuple(x for h in heads for x in (carry_sc[h], acc_sc[h]))
    state = lax.fori_loop(0, i // unroll, group, state)
    for h in heads:
        o_ref[:, hs(h)] = state[2 * h + 1].astype(BF16)


def _sb_attention(p, *, batch, seq, n_heads, head_dim, tq):
    t = p.shape[0]
    nq = seq // tq
    nh = ATTN_HEADS
    assert n_heads % nh == 0
    ng = n_heads // nh
    wide = nh * head_dim
    return pl.pallas_call(
        functools.partial(_sb_attn_kernel, unroll=ATTN_UNROLL, dh=head_dim),
        grid=(batch, ng, nq),
        in_specs=[
            pl.BlockSpec((tq, wide), lambda b, g, i: (b * nq + i, g)),
            pl.BlockSpec((seq, wide), lambda b, g, i: (b, ng + g)),
            pl.BlockSpec((seq, wide), lambda b, g, i: (b, 2 * ng + g)),
        ],
        out_specs=pl.BlockSpec((tq, wide), lambda b, g, i: (b * nq + i, g)),
        out_shape=jax.ShapeDtypeStruct((t, n_heads * head_dim), BF16),
        scratch_shapes=[pltpu.VMEM((nh, tq, 1), F32), pltpu.VMEM((nh, tq, head_dim), F32)],
        compiler_params=_cparams(("arbitrary", "arbitrary", "arbitrary")),
        name="sb_attention",
    )(p, p, p)


def _route_top2(logits):
    n_e = logits.shape[1]
    lane = lax.broadcasted_iota(jnp.int32, logits.shape, 1)
    m1 = jnp.max(logits, axis=-1, keepdims=True)
    i1 = jnp.min(jnp.where(logits == m1, lane, n_e), axis=-1, keepdims=True)
    rest = jnp.where(lane == i1, -jnp.inf, logits)
    m2 = jnp.max(rest, axis=-1, keepdims=True)
    i2 = jnp.min(jnp.where(rest == m2, lane, n_e), axis=-1, keepdims=True)
    t = jnp.exp(m2 - m1)
    w1 = 1.0 / (1.0 + t)
    w2 = t / (1.0 + t)
    out = jnp.where(lane == 0, i1.astype(F32), 0.0)
    out = jnp.where(lane == 1, i2.astype(F32), out)
    out = jnp.where(lane == 2, w1, out)
    out = jnp.where(lane == 3, w2, out)
    return out


def _merge_kernel(*refs, chunk, gdim, moe, dw):
    ysb_ref, su_ref, sv_ref = refs[:3]
    ga_refs = refs[3:3 + dw]
    gb_refs = refs[3 + dw:3 + 2 * dw]
    refs = refs[3 + 2 * dw:]
    x_ref, gate1_ref, shift2_ref, scale2_ref, g2_ref, ws_ref, bs_ref, pa_ref, pb_ref, wo_ref = refs[:10]
    if moe:
        wr_ref, xo_ref, h_ref, route_ref, ysg_sc, m_sc = refs[10:]
    else:
        xo_ref, h_ref, ysg_sc, m_sc = refs[10:]
    tm = x_ref.shape[0]
    width = ysb_ref.shape[1]
    n_groups = ws_ref.shape[0]
    row = lax.broadcasted_iota(jnp.int32, (chunk, chunk), 0)
    col = lax.broadcasted_iota(jnp.int32, (chunk, chunk), 1)
    causal = col <= row
    for g in range(n_groups):
        cs = slice(g * gdim, (g + 1) * gdim)
        wg = jnp.where(causal, ws_ref[g], 0.0).astype(BF16)
        for ch in range(tm // chunk):
            rs = slice(ch * chunk, (ch + 1) * chunk)
            mixed = jnp.dot(wg, sv_ref[rs, cs], preferred_element_type=F32) + bs_ref[g]
            ysg_sc[rs, cs] = (su_ref[rs, cs].astype(F32) * mixed).astype(BF16)
    for s in range(dw):
        sl = slice(s * width, (s + 1) * width)
        a = jnp.dot(ysb_ref[...], pa_ref[:, sl], preferred_element_type=F32)
        b = jnp.dot(ysg_sc[...], pb_ref[:, sl], preferred_element_type=F32)
        m_sc[:, sl] = (ga_refs[s][...].astype(F32) * a + gb_refs[s][...].astype(F32) * b).astype(BF16)
    o = jnp.dot(m_sc[...], wo_ref[...], preferred_element_type=F32)
    xn = x_ref[...] + gate1_ref[...] * o
    xo_ref[...] = xn
    y = xn * lax.rsqrt(jnp.mean(xn * xn, axis=-1, keepdims=True) + EPS)
    h = (y * g2_ref[...]) * (1.0 + scale2_ref[...]) + shift2_ref[...]
    h_ref[...] = h.astype(h_ref.dtype)
    if moe:
        wr = wr_ref[...]
        h_hi, w_hi = h.astype(BF16), wr.astype(BF16)
        h_lo = (h - h_hi.astype(F32)).astype(BF16)
        w_lo = (wr - w_hi.astype(F32)).astype(BF16)
        logits = (jnp.dot(h_hi, w_hi, preferred_element_type=F32)
                  + jnp.dot(h_lo, w_hi, preferred_element_type=F32)
                  + jnp.dot(h_hi, w_lo, preferred_element_type=F32))
        route_ref[...] = _route_top2(logits)


def _merge(ysb, p, x2, mod_l, norm2_g, w_s, bs_b, pa_bf, pb_bf, wo_bf, w_router, *, seq, tm, width):
    t, d = x2.shape
    n_groups, chunk, _ = w_s.shape
    gdim = width // n_groups
    tpb = seq // tm
    moe = w_router is not None
    dw = d // width
    vec = lambda which: pl.BlockSpec((None, None, 1, d), lambda i: (i // tpb, which, 0, 0))
    resident = lambda shape: pl.BlockSpec(shape, lambda i: (0,) * len(shape), pipeline_mode=pl.Buffered(1))
    pcol = lambda cb: pl.BlockSpec((tm, width), lambda i: (i, cb))
    in_specs = [
        pl.BlockSpec((tm, width), lambda i: (i, 0)),
        pcol(3),
        pcol(4),
        *[pcol(5 + s) for s in range(dw)],
        *[pcol(5 + dw + s) for s in range(dw)],
        pl.BlockSpec((tm, d), lambda i: (i, 0)),
        vec(2), vec(3), vec(4),
        resident((1, d)),
        resident((n_groups, chunk, chunk)),
        resident((n_groups, chunk, gdim)),
        resident((width, d)), resident((width, d)), resident((d, d)),
    ]
    args = [ysb, p, p, *([p] * (2 * dw)), x2, mod_l, mod_l, mod_l, norm2_g.reshape(1, d), w_s, bs_b,
            pa_bf, pb_bf, wo_bf]
    out_specs = [pl.BlockSpec((tm, d), lambda i: (i, 0)), pl.BlockSpec((tm, d), lambda i: (i, 0))]
    out_shape = [jax.ShapeDtypeStruct((t, d), F32), jax.ShapeDtypeStruct((t, d), F32 if moe else BF16)]
    if moe:
        n_e = w_router.shape[1]
        in_specs.append(resident((d, n_e)))
        args.append(w_router)
        out_specs.append(pl.BlockSpec((tm, n_e), lambda i: (i, 0)))
        out_shape.append(jax.ShapeDtypeStruct((t, n_e), F32))
    kern = functools.partial(_merge_kernel, chunk=chunk, gdim=gdim, moe=moe, dw=dw)
    return pl.pallas_call(
        kern,
        grid=(t // tm,),
        in_specs=in_specs,
        out_specs=out_specs,
        out_shape=out_shape,
        scratch_shapes=[pltpu.VMEM((tm, width), BF16), pltpu.VMEM((tm, d), BF16)],
        compiler_params=_cparams(("arbitrary",)),
        name="merge_moe" if moe else "merge_dense",
    )(*args)


def _swiglu_step(x_ref, w_refs, w_scs, o_ref, n_units, unit):
    w1_sc, w3_sc, w2_sc = w_scs

    def cast_weights():
        for w_ref, w_sc in zip(w_refs, w_scs):
            w_sc[...] = w_ref[...].astype(BF16)

    def rows(r0, m):
        xs = x_ref[pl.ds(r0, m), :]
        a = jnp.dot(xs, w1_sc[...], preferred_element_type=F32)
        b = jnp.dot(xs, w3_sc[...], preferred_element_type=F32)
        hh = (jax.nn.silu(a) * b).astype(BF16)
        o_ref[pl.ds(r0, m), :] += jnp.dot(hh, w2_sc[...], preferred_element_type=F32)

    if isinstance(n_units, int):
        total = n_units * unit
        block = min(FFN_BLOCK, total)
        assert total % block == 0
        cast_weights()
        for s in range(total // block):
            rows(s * block, block)
        return

    max_units = x_ref.shape[0] // unit
    assert max_units * unit <= 2 * FFN_BLOCK + 2 * LANES, "two blocks of little more than FFN_BLOCK rows cover a tile"
    for u in range(1, max_units + 1):
        @pl.when(n_units == u)
        def _(u=u):
            first = u if u * unit <= FFN_BLOCK else -(-u // 2)
            cast_weights()
            rows(0, first * unit)
            if u > first:
                rows(first * unit, (u - first) * unit)


def _ffn_dense_kernel(x_ref, w1_ref, w3_ref, w2_ref, res_ref, gate_ref, o_ref, w1_sc, w3_sc, w2_sc, *, unit):
    j = pl.program_id(1)

    @pl.when(j == 0)
    def _():
        o_ref[...] = jnp.zeros_like(o_ref)

    _swiglu_step(x_ref, (w1_ref, w3_ref, w2_ref), (w1_sc, w3_sc, w2_sc), o_ref, x_ref.shape[0] // unit, unit)

    @pl.when(j == pl.num_programs(1) - 1)
    def _():
        o_ref[...] = res_ref[...] + gate_ref[...] * o_ref[...]


def _ffn_dense(h, w1, w3, w2, res, mod_l, *, seq, tm, tf, unit):
    n, d = h.shape
    n_f = w1.shape[1]
    tpb = seq // tm
    return pl.pallas_call(
        functools.partial(_ffn_dense_kernel, unit=unit),
        grid=(n // tm, n_f // tf),
        in_specs=[
            pl.BlockSpec((tm, d), lambda i, j: (i, 0)),
            pl.BlockSpec((d, tf), lambda i, j: (0, j)),
            pl.BlockSpec((d, tf), lambda i, j: (0, j)),
            pl.BlockSpec((tf, d), lambda i, j: (j, 0)),
            pl.BlockSpec((tm, d), lambda i, j: (i, 0), pipeline_mode=pl.Buffered(1)),
            pl.BlockSpec((None, None, 1, d), lambda i, j: (i // tpb, 5, 0, 0)),
        ],
        out_specs=pl.BlockSpec((tm, d), lambda i, j: (i, 0)),
        out_shape=jax.ShapeDtypeStruct((n, d), F32),
        scratch_shapes=[pltpu.VMEM((d, tf), BF16), pltpu.VMEM((d, tf), BF16), pltpu.VMEM((tf, d), BF16)],
        compiler_params=_cparams(("arbitrary", "arbitrary")),
        name="ffn_dense",
    )(h, w1, w3, w2, res, mod_l)


def _row_copy(src_hbm, src_row, dst_ref, dst_row, sem):
    return pltpu.make_async_copy(src_hbm.at[pl.ds(src_row, 1), :], dst_ref.at[pl.ds(dst_row, 1), :], sem)


def _ffn_moe_kernel(te_ref, tv_ref, src_ref, x_hbm, w1_ref, w3_ref, w2_ref, o_ref, xg_sc, xb_sc, w1_sc, w3_sc,
                    w2_sc, sem, *, unit, rows_per_step):
    del te_ref
    i = pl.program_id(0)
    j = pl.program_id(1)
    tm = xb_sc.shape[0]
    n_units = tv_ref[i]

    def start_row(tile, r):
        _row_copy(x_hbm, src_ref[tile * tm + r], xg_sc, r, sem).start()

    @pl.when(jnp.logical_and(i == 0, j == 0))
    def _():
        lax.fori_loop(0, tv_ref[0] * unit, lambda r, c: (start_row(0, r), c)[1], 0)

    @pl.when(j == 0)
    def _():
        def wait_unit(u, c):
            r0 = pl.multiple_of(u * unit, unit)
            pltpu.make_async_copy(x_hbm.at[pl.ds(0, unit), :], xg_sc.at[pl.ds(r0, unit), :], sem).wait()
            return c

        lax.fori_loop(0, n_units, wait_unit, 0)

        def cast_unit(u, c):
            r0 = pl.multiple_of(u * unit, unit)
            xb_sc[pl.ds(r0, unit), :] = xg_sc[pl.ds(r0, unit), :].astype(BF16)
            return c

        lax.fori_loop(0, n_units, cast_unit, 0)
        o_ref[...] = jnp.zeros_like(o_ref)

    @pl.when(i + 1 < pl.num_programs(0))
    def _():
        n_next = tv_ref[i + 1] * unit
        for g in range(rows_per_step // DMA_GROUP):
            first = j * rows_per_step + g * DMA_GROUP

            @pl.when(first < n_next)
            def _(first=first):
                for r in range(DMA_GROUP):
                    start_row(i + 1, first + r)

    @pl.when(n_units > 0)
    def _():
        _swiglu_step(xb_sc, (w1_ref, w3_ref, w2_ref), (w1_sc, w3_sc, w2_sc), o_ref, n_units, unit)


def _ffn_moe(h, w1, w3, w2, tile_e, tile_v, src, *, tm, tf, unit):
    n_pad = src.shape[0]
    d = h.shape[1]
    n_f = w1.shape[2]
    nj = n_f // tf
    jj = lambda i, j, tv: jnp.where(tv[i] > 0, j, nj - 1)
    rows_per_step = -(-tm // (nj * DMA_GROUP)) * DMA_GROUP
    assert unit % DMA_GROUP == 0 and tm % unit == 0
    return pl.pallas_call(
        functools.partial(_ffn_moe_kernel, unit=unit, rows_per_step=rows_per_step),
        grid_spec=pltpu.PrefetchScalarGridSpec(
            num_scalar_prefetch=3,
            grid=(n_pad // tm, nj),
            in_specs=[
                pl.BlockSpec(memory_space=pl.ANY),
                pl.BlockSpec((None, d, tf), lambda i, j, te, tv, src: (te[i], 0, jj(i, j, tv))),
                pl.BlockSpec((None, d, tf), lambda i, j, te, tv, src: (te[i], 0, jj(i, j, tv))),
                pl.BlockSpec((None, tf, d), lambda i, j, te, tv, src: (te[i], jj(i, j, tv), 0)),
            ],
            out_specs=pl.BlockSpec((tm, d), lambda i, j, te, tv, src: (i, 0)),
            scratch_shapes=[pltpu.VMEM((tm, d), h.dtype), pltpu.VMEM((tm, d), BF16),
                            pltpu.VMEM((d, tf), BF16), pltpu.VMEM((d, tf), BF16), pltpu.VMEM((tf, d), BF16),
                            pltpu.SemaphoreType.DMA(())],
        ),
        out_shape=jax.ShapeDtypeStruct((n_pad, d), F32),
        compiler_params=_cparams(("arbitrary", "arbitrary")),
        name="ffn_moe",
    )(tile_e, tile_v, src, h, w1, w3, w2)


def _combine_kernel(pos0_ref, pos1_ref, ys_hbm, x_ref, gate_ref, w0_ref, w1_ref, o_ref, buf0, buf1, sem):
    tc = x_ref.shape[0]
    i = pl.program_id(0)

    def fetch(step):
        slot = step % 2
        base = step * tc

        def issue(g, c):
            for u in range(DMA_GROUP):
                r = g * DMA_GROUP + u
                _row_copy(ys_hbm, pos0_ref[base + r], buf0.at[slot], r, sem.at[slot]).start()
                _row_copy(ys_hbm, pos1_ref[base + r], buf1.at[slot], r, sem.at[slot]).start()
            return c

        lax.fori_loop(0, tc // DMA_GROUP, issue, 0)

    @pl.when(i == 0)
    def _():
        fetch(0)

    @pl.when(i + 1 < pl.num_programs(0))
    def _():
        fetch(i + 1)

    slot = i % 2
    for buf in (buf0, buf1):
        pltpu.make_async_copy(ys_hbm.at[pl.ds(0, tc), :], buf.at[slot], sem.at[slot]).wait()
    o_ref[...] = x_ref[...] + gate_ref[...] * (w0_ref[...] * buf0[slot] + w1_ref[...] * buf1[slot])


def _combine(pos0, pos1, ys, x2, mod_l, w0, w1, *, seq, tc):
    t, d = x2.shape
    tpb = seq // tc
    return pl.pallas_call(
        _combine_kernel,
        grid_spec=pltpu.PrefetchScalarGridSpec(
            num_scalar_prefetch=2,
            grid=(t // tc,),
            in_specs=[pl.BlockSpec(memory_space=pl.ANY),
                      pl.BlockSpec((tc, d), lambda i, p0, p1: (i, 0)),
                      pl.BlockSpec((None, None, 1, d), lambda i, p0, p1: (i // tpb, 5, 0, 0)),
                      pl.BlockSpec((tc, 1), lambda i, p0, p1: (i, 0)),
                      pl.BlockSpec((tc, 1), lambda i, p0, p1: (i, 0))],
            out_specs=pl.BlockSpec((tc, d), lambda i, p0, p1: (i, 0)),
            scratch_shapes=[pltpu.VMEM((2, tc, d), F32), pltpu.VMEM((2, tc, d), F32),
                            pltpu.SemaphoreType.DMA((2,))],
        ),
        out_shape=jax.ShapeDtypeStruct((t, d), F32),
        compiler_params=_cparams(("arbitrary",)),
        name="moe_combine",
    )(pos0, pos1, ys, x2, mod_l, w0, w1)


def _route_metadata(route, n_experts, tm, unit):
    t = route.shape[0]
    e = route[:, :TOP_K].astype(jnp.int32)
    w = route[:, TOP_K:2 * TOP_K]
    cnt = jnp.sum((e[:, :, None] == jnp.arange(n_experts, dtype=jnp.int32)).astype(jnp.int32), axis=1)
    csum = jnp.cumsum(cnt, axis=0)
    rank = csum - cnt
    counts = csum[-1]
    padded = ((counts + tm - 1) // tm) * tm
    ends = jnp.cumsum(padded)
    starts = ends - padded
    pos = starts[e] + jnp.take_along_axis(rank, e, axis=1)
    n_tiles = (TOP_K * t + n_experts * (tm - 1)) // tm
    n_pad = n_tiles * tm
    tok = jnp.broadcast_to(jnp.arange(t, dtype=jnp.int32)[:, None], (t, TOP_K))
    flat = pos.reshape(-1)
    src = jnp.zeros((n_pad,), jnp.int32).at[flat].set(tok.reshape(-1), unique_indices=True)
    tile_start = jnp.arange(n_tiles, dtype=jnp.int32) * tm
    tile_e = jnp.sum((tile_start[:, None] >= ends[None, :]).astype(jnp.int32), axis=1)
    tile_e = jnp.minimum(tile_e, n_experts - 1)
    real_rows = jnp.clip(starts[tile_e] + counts[tile_e] - tile_start, 0, tm)
    tile_v = ((real_rows + unit - 1) // unit).astype(jnp.int32)
    return pos[:, 0], pos[:, 1], w[:, 0:1], w[:, 1:2], src, tile_e, tile_v


def kernel(x, c, w_ada, b_ada, norm1_g, norm2_g, w_in, q_norm_g, k_norm_g, sg_norm_g, w_s, b_s, w_proj_sb,
           w_proj_sg, w_out, ffn_w1, ffn_w3, ffn_w2, moe_router, moe_w1, moe_w3, moe_w2):
    batch, seq, d = x.shape
    depth = w_ada.shape[0]
    head_dim = q_norm_g.shape[1]
    width = w_proj_sb.shape[1]
    n_heads = width // head_dim
    n_groups, chunk = w_s.shape[1], w_s.shape[2]
    gdim = width // n_groups
    n_experts = moe_router.shape[2]
    assert sg_norm_g.shape[1] == width and d % width == 0 and w_in.shape[2] == 5 * width + 2 * d
    assert head_dim % LANES == 0 and gdim % LANES == 0 and n_experts >= 2 * TOP_K
    cfg = _tiles(seq, d)
    t = batch * seq

    mod = _adaln(c, w_ada, b_ada, cfg["ada_tn"]).reshape(depth, batch, N_MOD, 1, d)
    x2 = x.reshape(t, d)
    for i in range(depth):
        mod_l = mod[i]
        p = _in_proj(x2, mod_l, norm1_g[i], w_in, i, q_norm_g[i], k_norm_g[i], sg_norm_g[i],
                     seq=seq, tm=cfg["in_tm"], width=width, head_dim=head_dim)
        ysb = _sb_attention(p, batch=batch, seq=seq, n_heads=n_heads, head_dim=head_dim, tq=cfg["attn_tq"])
        bs_b = jnp.broadcast_to(b_s[i][:, :, None], (n_groups, chunk, gdim))
        moe = i % 2 == 1
        outs = _merge(ysb, p, x2, mod_l, norm2_g[i], w_s[i], bs_b, w_proj_sb[i].astype(BF16),
                      w_proj_sg[i].astype(BF16), w_out[i].astype(BF16), moe_router[i // 2] if moe else None,
                      seq=seq, tm=cfg["merge_tm"], width=width)
        if not moe:
            x2, h = outs
            jd = i // 2
            x2 = _ffn_dense(h, ffn_w1[jd], ffn_w3[jd], ffn_w2[jd], x2, mod_l, seq=seq,
                            tm=cfg["ffn_tm"], tf=cfg["ffn_tf"], unit=cfg["ffn_unit"])
        else:
            x2, h, route = outs
            jm = i // 2
            pos0, pos1, w0, w1, src, tile_e, tile_v = _route_metadata(route, n_experts, cfg["moe_tm"],
                                                                      cfg["ffn_unit"])
            ys = _ffn_moe(h, moe_w1[jm], moe_w3[jm], moe_w2[jm], tile_e, tile_v, src,
                          tm=cfg["moe_tm"], tf=cfg["ffn_tf"], unit=cfg["ffn_unit"])
            x2 = _combine(pos0, pos1, ys, x2, mod_l, w0, w1, seq=seq, tc=cfg["gather_rows"])
    return x2.reshape(batch, seq, d)
```
